```python
import math
import jax
import jax.numpy as jnp
from jax import lax
import numpy as np

D_MODEL = 1024
BATCH = 4
SEQ = 4096
DEPTH = 4
DEC_BATCH = 128
DEC_SEQ = 1
PAST_LEN = 2048
PAGE_SIZE = 128

H_A = 4
DH_A = 64
H_B = 4
DK_B = 128
DV_B = 128
H_C = 4
DH_C = 64
W_A = H_A * DH_A
W_B = H_B * DV_B
W_C = H_C * DH_C
MIX_WIDTH = W_A + W_B + W_C
CONV_W = 4
CONV_CH = 2 * H_B * DK_B + W_B
DN_CHUNK = 64
QBLK = 128
CMP_BLOCK = 64
SEL_BLOCK = 64
TOP_K_BLOCKS = 16
WINDOW = 512
N_EXPERTS = 8
TOP_K = 2
D_FF = 2816
D_FF_E = 1408
N_DENSE = (DEPTH + 1) // 2
N_MOE = DEPTH // 2
EPS = 1e-6
NEG_INF = -1e30
IN_SPLITS = (
    ('a_q', W_A), ('a_k', W_A), ('a_v', W_A),
    ('b_q', H_B * DK_B), ('b_k', H_B * DK_B), ('b_v', W_B), ('b_z', W_B), ('b_beta', H_B), ('b_decay', H_B),
    ('c_q', W_C), ('c_kc', DH_C), ('c_vc', DH_C), ('c_ks', DH_C), ('c_vs', DH_C), ('c_kw', DH_C), ('c_vw', DH_C),
    ('c_gate', 3 * H_C),
)
N_IN = sum(width for _, width in IN_SPLITS)

kernel_name = 'hymba_sb_gdn_nsa_decode_step'


def rms_norm(x, g):
    xf = x.astype(jnp.float32)
    y = xf * lax.rsqrt(jnp.mean(xf * xf, axis=-1, keepdims=True) + EPS)
    return (y * g.astype(jnp.float32)).astype(x.dtype)


def l2_norm(x):
    xf = x.astype(jnp.float32)
    return xf * lax.rsqrt(jnp.sum(xf * xf, axis=-1, keepdims=True) + EPS)


def alibi_slopes(n):
    return 2.0 ** (-8.0 * jnp.arange(1, n + 1, dtype=jnp.float32) / n)


def split_columns(proj):
    out = {}
    start = 0
    for name, width in IN_SPLITS:
        out[name] = proj[..., start:start + width]
        start += width
    return out


def masked_softmax(s, mask):
    s = jnp.where(mask, s, NEG_INF)
    m = jnp.max(s, axis=-1, keepdims=True)
    e = jnp.where(mask, jnp.exp(s - m), 0.0)
    return e / jnp.maximum(jnp.sum(e, axis=-1, keepdims=True), 1e-30)


def sweep_query_blocks(fn, xs, q_pos):
    L = q_pos.shape[0]
    if L <= QBLK or L % QBLK:
        return fn(xs, q_pos)
    nb = L // QBLK
    xb = tuple(jnp.moveaxis(a.reshape(a.shape[0], nb, QBLK, *a.shape[2:]), 1, 0) for a in xs)
    pb = q_pos.reshape(nb, QBLK)
    out = lax.map(lambda args: fn(args[0], args[1]), (xb, pb))
    return jnp.moveaxis(out, 0, 1).reshape(out.shape[1], L, *out.shape[3:])


def stick_breaking(q, k, v, q_pos, k_pos):
    z = jnp.einsum('bqhd,bkhd->bhqk', q, k).astype(jnp.float32) * (DH_A ** -0.5)
    before = k_pos[None, :] < q_pos[:, None]
    log_keep = jnp.where(before, jax.nn.log_sigmoid(-z), 0.0)
    log_w = jax.nn.log_sigmoid(z) + lax.cumsum(log_keep, axis=3, reverse=True) - log_keep
    w = jnp.where(before, jnp.exp(log_w), 0.0)
    return jnp.einsum('bhqk,bkhd->bqhd', w.astype(v.dtype), v)


def gated_delta_rule(q, k, v, g, beta, s0):
    b, L, h, _ = q.shape
    dv = v.shape[-1]
    c = min(DN_CHUNK, L)
    n = -(-L // c)
    pad = n * c - L

    def chunks(x):
        x = jnp.pad(x.astype(jnp.float32), [(0, 0), (0, pad)] + [(0, 0)] * (x.ndim - 2))
        x = x.reshape(b, n, c, *x.shape[2:])
        return jnp.moveaxis(x, (1, 3), (0, 2))

    qc, kc, vc, gc, bc = (chunks(a) for a in (q, k, v, g, beta))
    G = jnp.cumsum(gc, axis=-1)
    idx = jnp.arange(c)
    lower = idx[:, None] >= idx[None, :]
    strict = idx[:, None] > idx[None, :]
    decay = jnp.exp(jnp.where(lower, G[..., :, None] - G[..., None, :], -jnp.inf))
    kbeta = kc * bc[..., None]
    a_mat = jnp.where(strict, jnp.einsum('nbhik,nbhjk->nbhij', kbeta, kc) * decay, 0.0)
    rhs = jnp.concatenate([vc * bc[..., None], kbeta * jnp.exp(G)[..., None]], axis=-1)
    sol = lax.linalg.triangular_solve(jnp.eye(c, dtype=jnp.float32) + a_mat, rhs, left_side=True, lower=True)
    u_base, w = sol[..., :dv], sol[..., dv:]
    qk = jnp.einsum('nbhik,nbhjk->nbhij', qc, kc) * decay
    q_dec = qc * jnp.exp(G)[..., None]
    k_dec = kc * jnp.exp(G[..., -1:] - G)[..., None]
    g_last = jnp.exp(G[..., -1])

    def step(S, xs):
        ub, wc, qkc, qd, kd, gl = xs
        u = ub - jnp.einsum('bhck,bhkv->bhcv', wc, S)
        o = jnp.einsum('bhck,bhkv->bhcv', qd, S) + jnp.einsum('bhij,bhjv->bhiv', qkc, u)
        S = S * gl[..., None, None] + jnp.einsum('bhck,bhcv->bhkv', kd, u)
        return S, o

    s_fin, o = lax.scan(step, s0, (u_base, w, qk, q_dec, k_dec, g_last))
    o = jnp.moveaxis(o, (0, 2), (1, 3)).reshape(b, n * c, h, dv)[:, :L]
    return o, s_fin


def block_mean(rows):
    b, n_rows, d = rows.shape
    nb = n_rows // CMP_BLOCK
    blocks = rows[:, :nb * CMP_BLOCK].reshape(b, nb, CMP_BLOCK, d)
    return jnp.mean(blocks.astype(jnp.float32), axis=2).astype(rows.dtype)


def to_blocks(rows, size):
    b, n_rows, d = rows.shape
    n = -(-n_rows // size)
    rows = jnp.pad(rows, ((0, 0), (0, n * size - n_rows), (0, 0)))
    return rows.reshape(b, n, size, d)


def nsa_attend(q, gates, q_pos, kcb, vcb, ksb, vsb, kw, vw, kw_pos):
    slopes = alibi_slopes(H_C)
    scale = DH_C ** -0.5
    qpf = q_pos.astype(jnp.float32)
    nbc = kcb.shape[1]
    blk = jnp.arange(nbc)
    blk_mid = (blk * CMP_BLOCK).astype(jnp.float32) + 0.5 * (CMP_BLOCK - 1)
    vis_c = ((blk + 1) * CMP_BLOCK - 1)[None, :] <= q_pos[:, None]
    s_c = (jnp.einsum('bqhd,bnd->bhqn', q, kcb).astype(jnp.float32) * scale
           - slopes[:, None, None] * (qpf[:, None] - blk_mid[None, :]))
    p_c = masked_softmax(s_c, vis_c)
    o_c = jnp.einsum('bhqn,bnd->bqhd', p_c.astype(vcb.dtype), vcb)
    nsb = ksb.shape[1]
    imp = jnp.pad(jnp.sum(p_c, axis=1), ((0, 0), (0, 0), (0, nsb - nbc)))
    j = jnp.arange(nsb)[None, :]
    cur = (q_pos // SEL_BLOCK)[:, None]
    forced = (j == 0) | (j == cur) | (j == cur - 1)
    score = jnp.where(j > cur, -1.0, jnp.where(forced, 2.0 * H_C, imp))
    _, sel = lax.top_k(score, min(TOP_K_BLOCKS, nsb))
    ks_sel = jax.vmap(lambda blocks, i: blocks[i])(ksb, sel)
    vs_sel = jax.vmap(lambda blocks, i: blocks[i])(vsb, sel)
    ks_pos = sel[..., None] * SEL_BLOCK + jnp.arange(SEL_BLOCK)
    s_s = (jnp.einsum('bqhd,bqksd->bhqks', q, ks_sel).astype(jnp.float32) * scale
           - slopes[:, None, None, None] * (qpf[None, None, :, None, None] - ks_pos[:, None].astype(jnp.float32)))
    mask_s = ks_pos[:, None] <= q_pos[None, None, :, None, None]
    bq, hq, lq, kk, sb = s_s.shape
    p_s = masked_softmax(s_s.reshape(bq, hq, lq, kk * sb), mask_s.reshape(bq, 1, lq, kk * sb))
    o_s = jnp.einsum('bhqks,bqksd->bqhd', p_s.reshape(bq, hq, lq, kk, sb).astype(vs_sel.dtype), vs_sel)
    vis_w = ((kw_pos[None, :] <= q_pos[:, None]) & (q_pos[:, None] - kw_pos[None, :] < WINDOW)
             & (kw_pos[None, :] >= 0))
    s_w = (jnp.einsum('bqhd,bkd->bhqk', q, kw).astype(jnp.float32) * scale
           - slopes[:, None, None] * (qpf[:, None] - kw_pos[None, :].astype(jnp.float32)))
    p_w = masked_softmax(s_w, vis_w)
    o_w = jnp.einsum('bhqk,bkd->bqhd', p_w.astype(vw.dtype), vw)
    return gates[..., 0:1] * o_c + gates[..., 1:2] * o_s + gates[..., 2:3] * o_w


def token_mixers(h, lw, past):
    b, L, _ = h.shape
    p_len = 0 if past is None else past[0].shape[1]
    q_pos = p_len + jnp.arange(L, dtype=jnp.int32)
    k_pos = jnp.arange(p_len + L, dtype=jnp.int32)
    c = split_columns(h @ lw['w_in'])

    qa = c['a_q'].reshape(b, L, H_A, DH_A)
    ka = c['a_k'].reshape(b, L, H_A, DH_A)
    va = c['a_v'].reshape(b, L, H_A, DH_A)
    sb_rows = jnp.stack([ka, va], axis=2)
    if past is None:
        ka_all, va_all = ka, va
    else:
        ka_all = jnp.concatenate([past[0][:, :, 0], ka], axis=1)
        va_all = jnp.concatenate([past[0][:, :, 1], va], axis=1)
    o_a = sweep_query_blocks(lambda xs, pos: stick_breaking(xs[0], ka_all, va_all, pos, k_pos), (qa,), q_pos)
    o_a = rms_norm(o_a, lw['a_out_norm'].reshape(H_A, DH_A)).reshape(b, L, W_A)

    qkv_raw = jnp.concatenate([c['b_q'], c['b_k'], c['b_v']], axis=-1)
    buf = jnp.zeros((b, CONV_W - 1, CONV_CH), h.dtype) if past is None else past[4]
    xpad = jnp.concatenate([buf, qkv_raw], axis=1)
    conv = xpad[:, 0:L] * lw['conv_w'][0]
    for i in range(1, CONV_W):
        conv = conv + xpad[:, i:i + L] * lw['conv_w'][i]
    qkv = jax.nn.silu(conv)
    new_conv = xpad[:, L:]
    nk = H_B * DK_B
    qb = l2_norm(qkv[..., :nk].reshape(b, L, H_B, DK_B)) * (DK_B ** -0.5)
    kb = l2_norm(qkv[..., nk:2 * nk].reshape(b, L, H_B, DK_B))
    vb = qkv[..., 2 * nk:].reshape(b, L, H_B, DV_B).astype(jnp.float32)
    beta = jax.nn.sigmoid(c['b_beta'].astype(jnp.float32))
    g = -jnp.exp(lw['a_log'].astype(jnp.float32)) * jax.nn.softplus(
        c['b_decay'].astype(jnp.float32) + lw['dt_bias'].astype(jnp.float32))
    s0 = jnp.zeros((b, H_B, DK_B, DV_B), jnp.float32) if past is None else past[3].astype(jnp.float32)
    o_b, s_new = gated_delta_rule(qb, kb, vb, g, beta, s0)
    z = c['b_z'].reshape(b, L, H_B, DV_B).astype(jnp.float32)
    o_b = (rms_norm(o_b, lw['b_out_norm']) * jax.nn.silu(z)).reshape(b, L, W_B).astype(h.dtype)

    qc = rms_norm(c['c_q'].reshape(b, L, H_C, DH_C), lw['c_q_norm'])
    ks = rms_norm(c['c_ks'], lw['c_k_norm'][1])
    kw = rms_norm(c['c_kw'], lw['c_k_norm'][2])
    nsa_rows = jnp.stack([c['c_kc'], c['c_vc'], ks, c['c_vs']], axis=2)
    win_rows = jnp.stack([kw, c['c_vw']], axis=2)
    gates = jax.nn.sigmoid(c['c_gate'].astype(jnp.float32)).reshape(b, L, H_C, 3).astype(h.dtype)
    nsa_all = nsa_rows if past is None else jnp.concatenate([past[1], nsa_rows], axis=1)
    kcb = rms_norm(block_mean(nsa_all[:, :, 0]), lw['c_k_norm'][0])
    vcb = block_mean(nsa_all[:, :, 1])
    ksb = to_blocks(nsa_all[:, :, 2], SEL_BLOCK)
    vsb = to_blocks(nsa_all[:, :, 3], SEL_BLOCK)
    if past is None:
        win_pad = jnp.pad(win_rows, ((0, 0), (WINDOW, 0), (0, 0), (0, 0)))

        def nsa_block(xs, pos):
            lq = pos.shape[0]
            band = lax.dynamic_slice_in_dim(win_pad, pos[0], WINDOW + lq, axis=1)
            band_pos = pos[0] - WINDOW + jnp.arange(WINDOW + lq, dtype=jnp.int32)
            return nsa_attend(xs[0], xs[1], pos, kcb, vcb, ksb, vsb, band[:, :, 0], band[:, :, 1], band_pos)

        o_c = sweep_query_blocks(nsa_block, (qc, gates), q_pos)
        new_win = win_rows[:, -min(WINDOW, L):]
    else:
        n_buf = past[2].shape[1]
        win_all = jnp.concatenate([past[2], win_rows], axis=1)
        win_pos = p_len - n_buf + jnp.arange(n_buf + L, dtype=jnp.int32)
        o_c = nsa_attend(qc, gates, q_pos, kcb, vcb, ksb, vsb, win_all[:, :, 0], win_all[:, :, 1], win_pos)
        new_win = win_all[:, -n_buf:]
    o_c = rms_norm(o_c, lw['c_out_norm'].reshape(H_C, DH_C)).reshape(b, L, W_C)

    mixed = jnp.concatenate([o_a, o_b, o_c], axis=-1) @ lw['w_out']
    return mixed, (sb_rows, nsa_rows, new_win, s_new.astype(h.dtype), new_conv)


def swiglu(x, w_gu, w_down):
    gate, up = jnp.split(x @ w_gu, 2, axis=-1)
    return (jax.nn.silu(gate) * up) @ w_down


def moe_ffn(x, w_r, b_r, w_gu, w_down):
    logits = (x @ w_r).astype(jnp.float32) + b_r.astype(jnp.float32)
    top_v, top_i = lax.top_k(logits, TOP_K)
    gate = jax.nn.softmax(top_v, axis=-1)
    dense_gate = jnp.sum(jax.nn.one_hot(top_i, N_EXPERTS, dtype=jnp.float32) * gate[..., None], axis=-2)
    y = jnp.zeros_like(x)
    for e in range(N_EXPERTS):
        y = y + dense_gate[..., e:e + 1].astype(x.dtype) * swiglu(x, w_gu[e], w_down[e])
    return y


def decoder_layer(x, l, lw, past, dense_w_gu, dense_w_down, router_w, router_b, moe_w_gu, moe_w_down):
    mixed, states = token_mixers(rms_norm(x, lw['attn_norm']), lw, past)
    x = x + mixed
    h = rms_norm(x, lw['ffn_norm'])
    i = l // 2
    if l % 2 == 0:
        x = x + swiglu(h, dense_w_gu[i], dense_w_down[i])
    else:
        x = x + moe_ffn(h, router_w[i], router_b[i], moe_w_gu[i], moe_w_down[i])
    return x, states


def setup_inputs(seed: int = 0) -> dict:
    key = jax.random.key(seed)
    ks = jax.random.split(key, 32)
    f32 = jnp.float32
    n_pages = PAST_LEN // PAGE_SIZE
    n_used = DEC_BATCH * n_pages
    n_phys = n_used + (n_used + 3) // 4
    w_buf = min(WINDOW, PAST_LEN)

    def nrm(k, shape, scale):
        return jax.random.normal(k, shape, f32) * scale

    def gain(k, shape):
        return 1.0 + 0.05 * jax.random.normal(k, shape, f32)

    perm = jax.random.permutation(ks[7], n_phys)
    page_table = perm[:n_used].reshape(DEC_BATCH, n_pages).astype(jnp.int32)
    dt = jnp.exp(jax.random.uniform(ks[9], (DEPTH, H_B), f32, math.log(1e-3), math.log(1e-1)))
    dt_bias = dt + jnp.log(-jnp.expm1(-dt))
    a_log = jnp.log(jax.random.uniform(ks[10], (DEPTH, H_B), f32, 1.0, 16.0))
    return {
        'x_prompt': nrm(ks[0], (BATCH, SEQ, D_MODEL), 1.0),
        'x_sample': nrm(ks[1], (DEC_BATCH, DEC_SEQ, D_MODEL), 1.0),
        'cache_sb': nrm(ks[2], (n_phys, DEPTH, PAGE_SIZE, 2, H_A, DH_A), 1.0),
        'cache_nsa': nrm(ks[3], (n_phys, DEPTH, PAGE_SIZE, 4, DH_C), 1.0),
        'state_win': nrm(ks[4], (DEC_BATCH, DEPTH, w_buf, 2, DH_C), 1.0),
        'state_delta': nrm(ks[5], (DEC_BATCH, DEPTH, H_B, DK_B, DV_B), 0.1),
        'state_conv': nrm(ks[6], (DEC_BATCH, DEPTH, CONV_W - 1, CONV_CH), 1.0),
        'page_table': page_table,
        'w_in': nrm(ks[8], (DEPTH, D_MODEL, N_IN), D_MODEL ** -0.5),
        'conv_w': nrm(ks[11], (DEPTH, CONV_W, CONV_CH), 0.5),
        'a_log': a_log,
        'dt_bias': dt_bias,
        'b_out_norm': gain(ks[12], (DEPTH, DV_B)),
        'a_out_norm': gain(ks[13], (DEPTH, W_A)),
        'c_q_norm': gain(ks[14], (DEPTH, DH_C)),
        'c_k_norm': gain(ks[15], (DEPTH, 3, DH_C)),
        'c_out_norm': gain(ks[16], (DEPTH, W_C)),
        'w_out': nrm(ks[17], (DEPTH, MIX_WIDTH, D_MODEL), MIX_WIDTH ** -0.5),
        'attn_norm': gain(ks[18], (DEPTH, D_MODEL)),
        'ffn_norm': gain(ks[19], (DEPTH, D_MODEL)),
        'dense_w_gu': nrm(ks[20], (N_DENSE, D_MODEL, 2 * D_FF), D_MODEL ** -0.5),
        'dense_w_down': nrm(ks[21], (N_DENSE, D_FF, D_MODEL), D_FF ** -0.5),
        'router_w': nrm(ks[22], (N_MOE, D_MODEL, N_EXPERTS), D_MODEL ** -0.5),
        'router_b': nrm(ks[23], (N_MOE, N_EXPERTS), 0.01),
        'moe_w_gu': nrm(ks[24], (N_MOE, N_EXPERTS, D_MODEL, 2 * D_FF_E), D_MODEL ** -0.5),
        'moe_w_down': nrm(ks[25], (N_MOE, N_EXPERTS, D_FF_E, D_MODEL), D_FF_E ** -0.5),
    }


def reference(x_prompt, x_sample, cache_sb, cache_nsa, state_win, state_delta, state_conv, page_table,
              w_in, conv_w, a_log, dt_bias, b_out_norm, a_out_norm, c_q_norm, c_k_norm, c_out_norm, w_out,
              attn_norm, ffn_norm, dense_w_gu, dense_w_down, router_w, router_b, moe_w_gu, moe_w_down):
    dec_batch, n_pages = page_table.shape
    past_len = n_pages * PAGE_SIZE
    xp, xs = x_prompt, x_sample
    st_p, st_s = [], []
    for l in range(DEPTH):
        lw = {'w_in': w_in[l], 'conv_w': conv_w[l], 'a_log': a_log[l], 'dt_bias': dt_bias[l],
              'b_out_norm': b_out_norm[l], 'a_out_norm': a_out_norm[l], 'c_q_norm': c_q_norm[l],
              'c_k_norm': c_k_norm[l], 'c_out_norm': c_out_norm[l], 'w_out': w_out[l],
              'attn_norm': attn_norm[l], 'ffn_norm': ffn_norm[l]}
        sb_past = cache_sb[page_table, l].reshape(dec_batch, past_len, 2, H_A, DH_A)
        nsa_past = cache_nsa[page_table, l].reshape(dec_batch, past_len, 4, DH_C)
        past = (sb_past, nsa_past, state_win[:, l], state_delta[:, l], state_conv[:, l])
        xp, sp = decoder_layer(xp, l, lw, None, dense_w_gu, dense_w_down, router_w, router_b, moe_w_gu, moe_w_down)
        xs, ss = decoder_layer(xs, l, lw, past, dense_w_gu, dense_w_down, router_w, router_b, moe_w_gu, moe_w_down)
        st_p.append(sp)
        st_s.append(ss)
    sb_p, nsa_p, win_p, delta_p, conv_p = (jnp.stack([s[i] for s in st_p], axis=1) for i in range(5))
    sb_s, nsa_s, win_s, delta_s, conv_s = (jnp.stack([s[i] for s in st_s], axis=1) for i in range(5))
    return (xp, xs, sb_p, sb_s, nsa_p, nsa_s, win_p, win_s, delta_p, delta_s, conv_p, conv_s)
```

```python
import functools

import jax
import jax.numpy as jnp
from jax import lax
from jax.experimental import pallas as pl
from jax.experimental.pallas import tpu as pltpu

F32 = jnp.float32
BF16 = jnp.bfloat16

D_MODEL = 1024
H_A, DH_A = 4, 64
H_B, DK_B, DV_B = 4, 128, 128
H_C, DH_C = 4, 64
W_A = H_A * DH_A
W_B = H_B * DV_B
W_C = H_C * DH_C
CONV_W = 4
CONV_CH = 2 * H_B * DK_B + W_B
DN_CHUNK = 64
CMP_BLOCK = 64
SEL_BLOCK = 64
TOP_K_BLOCKS = 16
WINDOW = 512
N_EXPERTS = 8
EPS = 1e-6
NEG_INF = -1e30

LANES = 128
SUBLANES = 8
VMEM_LIMIT = 56 * 1024 * 1024

N_PROJ = 512 + 256 + CONV_CH + W_B + W_C + 256 + 128 + 128
SMALL_BETA, SMALL_DECAY, SMALL_GATE = 0, H_B, 2 * H_B
TQ = 128
DN_BLOCK = 256


def _cparams(sem):
    return pltpu.CompilerParams(dimension_semantics=sem, vmem_limit_bytes=VMEM_LIMIT)


def _split2(x):
    hi = x.astype(BF16)
    lo = (x - hi.astype(F32)).astype(BF16)
    return hi, lo


def _split3(x):
    hi = x.astype(BF16)
    r = x - hi.astype(F32)
    mid = r.astype(BF16)
    lo = (r - mid.astype(F32)).astype(BF16)
    return hi, mid, lo


_NN = (((1,), (0,)), ((), ()))
_NT = (((1,), (1,)), ((), ()))
_TN = (((0,), (0,)), ((), ()))


def _mm(a, b, dims=_NN):
    return lax.dot_general(a, b, dims, preferred_element_type=F32)


def _mm_x01(x, m01, parts=2):
    ps = _split2(x) if parts == 2 else _split3(x)
    out = _mm(ps[0], m01)
    for p in ps[1:]:
        out = out + _mm(p, m01)
    return out


def _mm_01x(m01, x, parts=2):
    ps = _split2(x) if parts == 2 else _split3(x)
    out = _mm(m01, ps[0])
    for p in ps[1:]:
        out = out + _mm(m01, p)
    return out


def _mm3(a, b, dims=_NN):
    ah, al = _split2(a)
    bh, bl = _split2(b)
    return _mm(ah, bh, dims) + (_mm(ah, bl, dims) + _mm(al, bh, dims))


def _silu(x):
    return x * (1.0 / (1.0 + jnp.exp(-x)))


def _sigmoid(x):
    return 1.0 / (1.0 + jnp.exp(-x))


def _softplus(x):
    return jnp.maximum(x, 0.0) + jnp.log(1.0 + jnp.exp(-jnp.abs(x)))


def _group_ones(n, group):
    r = lax.broadcasted_iota(jnp.int32, (n, n), 0) // group
    c = lax.broadcasted_iota(jnp.int32, (n, n), 1) // group
    return (r == c).astype(BF16)


def _inproj_kernel(x_ref, g_ref, w_ref, qn_ref, kn_ref,
                   sb_ref, aq_ref, akv_ref, braw_ref, bz_ref, small_ref, cq_ref, nsa_ref, win_ref, ckv_ref):
    x = x_ref[...]
    h = x * lax.rsqrt(jnp.mean(x * x, axis=-1, keepdims=True) + EPS) * g_ref[...]
    p = _mm(h.astype(BF16), w_ref[...])
    o = 0
    sb = p[:, o:o + 512]; o += 512
    aq = p[:, o:o + 256]; o += 256
    braw = p[:, o:o + CONV_CH]; o += CONV_CH
    bz = p[:, o:o + W_B]; o += W_B
    cq = p[:, o:o + W_C]; o += W_C
    nsa = p[:, o:o + 256]; o += 256
    win = p[:, o:o + 128]; o += 128
    small = p[:, o:o + 128]
    sb_ref[...] = sb
    aq_ref[...] = (aq * (DH_A ** -0.5)).astype(BF16)
    akv_ref[...] = sb.astype(BF16)
    braw_ref[...] = braw
    bz_ref[...] = bz
    small_ref[...] = small
    g64 = _group_ones(256, DH_C)
    ms = _mm_x01(cq * cq, g64) * (1.0 / DH_C)
    cqn = cq * lax.rsqrt(ms + EPS) * qn_ref[...]
    cq_ref[...] = (cqn * (DH_C ** -0.5)).astype(BF16)
    lane = lax.broadcasted_iota(jnp.int32, nsa.shape, 1)
    is_ks = (lane >= 128) & (lane < 192)
    ms = _mm_x01(nsa * nsa, g64) * (1.0 / DH_C)
    nsa_n = jnp.where(is_ks, nsa * lax.rsqrt(ms + EPS) * kn_ref[:, 0:256], nsa)
    nsa_ref[...] = nsa_n
    lane = lax.broadcasted_iota(jnp.int32, win.shape, 1)
    ms = _mm_x01(win * win, g64[0:128, 0:128]) * (1.0 / DH_C)
    win_n = jnp.where(lane < 64, win * lax.rsqrt(ms + EPS) * kn_ref[:, 256:384], win)
    win_ref[...] = win_n
    ckv_ref[...] = jnp.concatenate([nsa_n[:, 128:256], win_n], axis=1).astype(BF16)


def _inproj(x, g, w, qn, kn, tm):
    t = x.shape[0]
    row = lambda i: (i, 0)
    const = lambda i: (0, 0)
    outs = [(512, F32), (256, BF16), (512, BF16), (CONV_CH, F32), (W_B, F32), (128, F32),
            (W_C, BF16), (256, F32), (128, F32), (256, BF16)]
    return pl.pallas_call(
        _inproj_kernel,
        grid=(t // tm,),
        in_specs=[pl.BlockSpec((tm, D_MODEL), row), pl.BlockSpec((1, D_MODEL), const),
                  pl.BlockSpec((D_MODEL, N_PROJ), const), pl.BlockSpec((1, 256), const),
                  pl.BlockSpec((1, 384), const)],
        out_specs=[pl.BlockSpec((tm, n), row) for n, _ in outs],
        out_shape=[jax.ShapeDtypeStruct((t, n), d) for n, d in outs],
        compiler_params=_cparams(("parallel",)),
        name="inproj",
    )(x, g, w, qn, kn)


def _mix_prologue(x_ref, oa_ref, ob_ref, oc_ref, wo_ref, g_ref, xn_scr, h_scr):
    mix = (_mm(oa_ref[...], wo_ref[0:W_A, :]) + _mm(ob_ref[...], wo_ref[W_A:W_A + W_B, :])
           + _mm(oc_ref[...], wo_ref[W_A + W_B:, :]))
    xn = x_ref[...] + mix
    xn_scr[...] = xn
    h = xn * lax.rsqrt(jnp.mean(xn * xn, axis=-1, keepdims=True) + EPS) * g_ref[...]
    h_scr[...] = h.astype(BF16)
    return h


def _ffn_dense_kernel(x_ref, oa_ref, ob_ref, oc_ref, wo_ref, g_ref, wg_ref, wu_ref, wd_ref, o_ref,
                      xn_scr, h_scr, acc_scr):
    f = pl.program_id(1)

    @pl.when(f == 0)
    def _():
        _mix_prologue(x_ref, oa_ref, ob_ref, oc_ref, wo_ref, g_ref, xn_scr, h_scr)
        acc_scr[...] = jnp.zeros_like(acc_scr)

    h = h_scr[...]
    act = _silu(_mm(h, wg_ref[...])) * _mm(h, wu_ref[...])
    acc_scr[...] += _mm(act.astype(BF16), wd_ref[...])

    @pl.when(f == pl.num_programs(1) - 1)
    def _():
        o_ref[...] = xn_scr[...] + acc_scr[...]


def _ffn_dense(x, oa, ob, oc, wo, g, wgu, wd, tm, tf):
    t = x.shape[0]
    d_ff = wd.shape[0]
    nf = d_ff // tf
    row = lambda i, f: (i, 0)
    const = lambda i, f: (0, 0)
    return pl.pallas_call(
        _ffn_dense_kernel,
        grid=(t // tm, nf),
        in_specs=[pl.BlockSpec((tm, D_MODEL), row), pl.BlockSpec((tm, W_A), row), pl.BlockSpec((tm, W_B), row),
                  pl.BlockSpec((tm, W_C), row), pl.BlockSpec((D_MODEL, D_MODEL), const),
                  pl.BlockSpec((1, D_MODEL), const),
                  pl.BlockSpec((D_MODEL, tf), lambda i, f: (0, f)),
                  pl.BlockSpec((D_MODEL, tf), lambda i, f: (0, f + nf)),
                  pl.BlockSpec((tf, D_MODEL), lambda i, f: (f, 0))],
        out_specs=pl.BlockSpec((tm, D_MODEL), row),
        out_shape=jax.ShapeDtypeStruct((t, D_MODEL), F32),
        scratch_shapes=[pltpu.VMEM((tm, D_MODEL), F32), pltpu.VMEM((tm, D_MODEL), BF16),
                        pltpu.VMEM((tm, D_MODEL), F32)],
        compiler_params=_cparams(("parallel", "arbitrary")),
        name="ffn_dense",
    )(x, oa, ob, oc, wo, g, wgu, wgu, wd)


def _ffn_moe_kernel(x_ref, oa_ref, ob_ref, oc_ref, wo_ref, g_ref, wr_ref, br_ref, wg_ref, wu_ref, wd_ref, o_ref,
                    xn_scr, h_scr, acc_scr, gate_scr):
    e = pl.program_id(1)

    @pl.when(e == 0)
    def _():
        h = _mix_prologue(x_ref, oa_ref, ob_ref, oc_ref, wo_ref, g_ref, xn_scr, h_scr)
        acc_scr[...] = jnp.zeros_like(acc_scr)
        logits = _mm3(h, wr_ref[...]) + br_ref[...]
        lane = lax.broadcasted_iota(jnp.int32, logits.shape, 1)
        logits = jnp.where(lane < N_EXPERTS, logits, -jnp.inf)
        m1 = jnp.max(logits, axis=-1, keepdims=True)
        i1 = jnp.min(jnp.where(logits == m1, lane, LANES), axis=-1, keepdims=True)
        rest = jnp.where(lane == i1, -jnp.inf, logits)
        m2 = jnp.max(rest, axis=-1, keepdims=True)
        i2 = jnp.min(jnp.where(rest == m2, lane, LANES), axis=-1, keepdims=True)
        e2 = jnp.exp(m2 - m1)
        den = 1.0 + e2
        gate_scr[...] = jnp.where(lane == i1, 1.0 / den, 0.0) + jnp.where(lane == i2, e2 / den, 0.0)

    h = h_scr[...]
    act = _silu(_mm(h, wg_ref[...])) * _mm(h, wu_ref[...])
    y = _mm(act.astype(BF16), wd_ref[...])
    lane = lax.broadcasted_iota(jnp.int32, gate_scr.shape, 1)
    ge = jnp.sum(jnp.where(lane == e, gate_scr[...], 0.0), axis=-1, keepdims=True)
    acc_scr[...] += ge * y

    @pl.when(e == pl.num_programs(1) - 1)
    def _():
        o_ref[...] = xn_scr[...] + acc_scr[...]


def _ffn_moe(x, oa, ob, oc, wo, g, wr, br, wgu, wd, tm):
    t = x.shape[0]
    ne, _, two_f = wgu.shape
    f = two_f // 2
    row = lambda i, e: (i, 0)
    const = lambda i, e: (0, 0)
    return pl.pallas_call(
        _ffn_moe_kernel,
        grid=(t // tm, ne),
        in_specs=[pl.BlockSpec((tm, D_MODEL), row), pl.BlockSpec((tm, W_A), row), pl.BlockSpec((tm, W_B), row),
                  pl.BlockSpec((tm, W_C), row), pl.BlockSpec((D_MODEL, D_MODEL), const),
                  pl.BlockSpec((1, D_MODEL), const), pl.BlockSpec((D_MODEL, LANES), const),
                  pl.BlockSpec((1, LANES), const),
                  pl.BlockSpec((None, D_MODEL, f), lambda i, e: (e, 0, 0)),
                  pl.BlockSpec((None, D_MODEL, f), lambda i, e: (e, 0, 1)),
                  pl.BlockSpec((None, f, D_MODEL), lambda i, e: (e, 0, 0))],
        out_specs=pl.BlockSpec((tm, D_MODEL), row),
        out_shape=jax.ShapeDtypeStruct((t, D_MODEL), F32),
        scratch_shapes=[pltpu.VMEM((tm, D_MODEL), F32), pltpu.VMEM((tm, D_MODEL), BF16),
                        pltpu.VMEM((tm, D_MODEL), F32), pltpu.VMEM((tm, LANES), F32)],
        compiler_params=_cparams(("parallel", "arbitrary")),
        name="ffn_moe",
    )(x, oa, ob, oc, wo, g, wr, br, wgu, wgu, wd)


def _sb_prompt_kernel(q_ref, kv_ref, gn_ref, o_ref):
    i = pl.program_id(1)
    r = lax.broadcasted_iota(jnp.int32, (TQ, TQ), 0)
    c = lax.broadcasted_iota(jnp.int32, (TQ, TQ), 1)
    before = c < r
    later01 = (r > c).astype(BF16)

    outs = []
    for h in range(H_A):
        q = q_ref[:, h * DH_A:(h + 1) * DH_A]

        def tile(j, carry, diag, h=h, q=q):
            run, acc = carry
            start = pl.multiple_of(j * TQ, TQ)
            k = kv_ref[pl.ds(start, TQ), h * DH_A:(h + 1) * DH_A]
            v = kv_ref[pl.ds(start, TQ), W_A + h * DH_A:W_A + (h + 1) * DH_A]
            z = _mm(q, k, _NT)
            sp = jnp.log(1.0 + jnp.exp(-jnp.abs(z)))
            ls = jnp.minimum(z, 0.0) - sp
            lk = ls - z
            if diag:
                lk = jnp.where(before, lk, 0.0)
            suffix = _mm_x01(lk, later01)
            w = jnp.exp(ls + suffix + run)
            if diag:
                w = jnp.where(before, w, 0.0)
            acc = acc + _mm(w.astype(BF16), v)
            run = run + jnp.sum(lk, axis=-1, keepdims=True)
            return run, acc

        carry = (jnp.zeros((TQ, 1), F32), jnp.zeros((TQ, DH_A), F32))
        carry = tile(i, carry, True)
        carry = lax.fori_loop(0, i, lambda jj, cr: tile(i - 1 - jj, cr, False), carry)
        o = carry[1]
        o = o * lax.rsqrt(jnp.mean(o * o, axis=-1, keepdims=True) + EPS)
        outs.append(o)
    o_ref[...] = (jnp.concatenate(outs, axis=1) * gn_ref[...]).astype(BF16)


def _sb_prompt(aq, akv, gn, batch, seq):
    nq = seq // TQ
    return pl.pallas_call(
        _sb_prompt_kernel,
        grid=(batch, nq),
        in_specs=[pl.BlockSpec((TQ, W_A), lambda b, i: (b * nq + i, 0)),
                  pl.BlockSpec((seq, 2 * W_A), lambda b, i: (b, 0)),
                  pl.BlockSpec((1, W_A), lambda b, i: (0, 0))],
        out_specs=pl.BlockSpec((TQ, W_A), lambda b, i: (b * nq + i, 0)),
        out_shape=jax.ShapeDtypeStruct((batch * seq, W_A), BF16),
        compiler_params=_cparams(("parallel", "arbitrary")),
        name="sb_prompt",
    )(aq, akv, gn)


def _page_copies(pt_ref, cache_ref, buf_ref, sem_ref, layer, seq_idx, slot, n_pages, page):
    return [pltpu.make_async_copy(cache_ref.at[pt_ref[seq_idx, p], layer],
                                  buf_ref.at[slot, pl.ds(p * page, page)], sem_ref.at[slot])
            for p in range(n_pages)]


def _sb_decode_kernel(pt_ref, q_ref, gn_ref, cache_ref, o_ref, buf_ref, sem_ref, *, layer, n_pages, page):
    s = pl.program_id(0)
    ns = pl.num_programs(0)
    slot = s % 2

    @pl.when(s == 0)
    def _():
        for cp in _page_copies(pt_ref, cache_ref, buf_ref, sem_ref, layer, 0, 0, n_pages, page):
            cp.start()

    @pl.when(s + 1 < ns)
    def _():
        for cp in _page_copies(pt_ref, cache_ref, buf_ref, sem_ref, layer, s + 1, 1 - slot, n_pages, page):
            cp.start()

    for cp in _page_copies(pt_ref, cache_ref, buf_ref, sem_ref, layer, s, slot, n_pages, page):
        cp.wait()

    q = q_ref[pl.ds(s, 1), :]
    rr = lax.broadcasted_iota(jnp.int32, (LANES, W_A), 0)
    cc = lax.broadcasted_iota(jnp.int32, (LANES, W_A), 1)
    qrows = jnp.where(rr == cc // DH_A, q, 0.0).astype(BF16)
    r = lax.broadcasted_iota(jnp.int32, (page, page), 0)
    c = lax.broadcasted_iota(jnp.int32, (page, page), 1)
    later01 = (c > r).astype(BF16)
    er = lax.broadcasted_iota(jnp.int32, (LANES, W_A), 0)
    ec = lax.broadcasted_iota(jnp.int32, (LANES, W_A), 1)
    expand01 = (er == ec // DH_A).astype(BF16)

    def page_step(pp, carry):
        run, acc = carry
        p = n_pages - 1 - pp
        start = pl.multiple_of(p * page, page)
        k = buf_ref[slot, pl.ds(start, page), 0:W_A]
        v = buf_ref[slot, pl.ds(start, page), W_A:2 * W_A]
        z = _mm(k.astype(BF16), qrows, _NT)
        sp = jnp.log(1.0 + jnp.exp(-jnp.abs(z)))
        ls = jnp.minimum(z, 0.0) - sp
        lk = ls - z
        suffix = _mm_01x(later01, lk)
        w = jnp.exp(ls + suffix + run)
        wexp = _mm(w.astype(BF16), expand01)
        acc = acc + jnp.sum(wexp * v, axis=0, keepdims=True)
        run = run + jnp.sum(lk, axis=0, keepdims=True)
        return run, acc

    run0 = jnp.zeros((1, LANES), F32)
    acc0 = jnp.zeros((1, W_A), F32)
    _, o = lax.fori_loop(0, n_pages, page_step, (run0, acc0))
    ms = _mm_x01(o * o, _group_ones(W_A, DH_A)) * (1.0 / DH_A)
    o_ref[pl.ds(s, 1), :] = o * lax.rsqrt(ms + EPS) * gn_ref[...]


def _sb_decode(page_table, aq, gn, cache, layer):
    n_seq, n_pages = page_table.shape
    page = cache.shape[2]
    kern = functools.partial(_sb_decode_kernel, layer=layer, n_pages=n_pages, page=page)
    return pl.pallas_call(
        kern,
        grid_spec=pltpu.PrefetchScalarGridSpec(
            num_scalar_prefetch=1,
            grid=(n_seq,),
            in_specs=[pl.BlockSpec((n_seq, W_A), lambda s, pt: (0, 0)),
                      pl.BlockSpec((1, W_A), lambda s, pt: (0, 0)),
                      pl.BlockSpec(memory_space=pl.ANY)],
            out_specs=pl.BlockSpec((n_seq, W_A), lambda s, pt: (0, 0)),
            scratch_shapes=[pltpu.VMEM((2, n_pages * page, 2 * W_A), F32),
                            pltpu.SemaphoreType.DMA((2,))]),
        out_shape=jax.ShapeDtypeStruct((n_seq, W_A), F32),
        compiler_params=_cparams(("arbitrary",)),
        name="sb_decode",
    )(page_table, aq, gn, cache)


def _unit_lower_inverse(a_mat, n):
    r = lax.broadcasted_iota(jnp.int32, (n, n), 0)
    c = lax.broadcasted_iota(jnp.int32, (n, n), 1)
    eye = (r == c).astype(F32)
    base = 16
    x = jnp.where(r // base == c // base, -a_mat, 0.0)
    inv = eye + x
    span = 1
    while 2 * span < base:
        x = _mm3(x, x)
        inv = _mm3(inv, eye + x)
        span *= 2
    size = base
    while size < DN_CHUNK:
        off = jnp.where((r // (2 * size) == c // (2 * size)) & (r // size != c // size), a_mat, 0.0)
        inv = inv - _mm3(inv, _mm3(off, inv))
        size *= 2
    return inv


def _l2n(x):
    return x * lax.rsqrt(jnp.sum(x * x, axis=-1, keepdims=True) + EPS)


def _dn_prompt_kernel(raw_ref, z_ref, small_ref, cw_ref, par_ref, gn_ref, o_ref, s_ref, xbuf, s_scr):
    ci = pl.program_id(1)
    tb = DN_BLOCK
    nk = H_B * DK_B

    @pl.when(ci == 0)
    def _():
        xbuf[0:SUBLANES, :] = jnp.zeros((SUBLANES, CONV_CH), F32)
        s_scr[...] = jnp.zeros_like(s_scr)

    xbuf[SUBLANES:SUBLANES + tb, :] = raw_ref[...]
    conv = xbuf[SUBLANES - 3:SUBLANES - 3 + tb, :] * cw_ref[0:1, :]
    for i in range(1, CONV_W):
        conv = conv + xbuf[SUBLANES - 3 + i:SUBLANES - 3 + i + tb, :] * cw_ref[i:i + 1, :]
    xbuf[0:SUBLANES, :] = xbuf[tb:tb + SUBLANES, :]
    qkv = _silu(conv)

    small = small_ref[...]
    beta_all = _sigmoid(small)
    g_all = -jnp.exp(par_ref[0:1, :]) * _softplus(small + par_ref[1:2, :])

    r = lax.broadcasted_iota(jnp.int32, (tb, tb), 0)
    c = lax.broadcasted_iota(jnp.int32, (tb, tb), 1)
    same = (r // DN_CHUNK) == (c // DN_CHUNK)
    lower = same & (r >= c)
    strict = same & (r > c)
    lower01 = lower.astype(BF16)
    upper01 = (same & (r <= c)).astype(BF16)
    same01 = same.astype(BF16)
    ones8 = jnp.ones((SUBLANES, tb), BF16)

    g_cum = _mm_01x(lower01, g_all, parts=3)
    g_tot = _mm_01x(same01, g_all, parts=3)
    e_cum = jnp.exp(g_cum)
    e_rest = jnp.exp(g_tot - g_cum)
    e_tot = jnp.exp(g_tot)

    for h in range(H_B):
        q = _l2n(qkv[:, h * DK_B:(h + 1) * DK_B]) * (DK_B ** -0.5)
        k = _l2n(qkv[:, nk + h * DK_B:nk + (h + 1) * DK_B])
        v = qkv[:, 2 * nk + h * DV_B:2 * nk + (h + 1) * DV_B]
        beta = beta_all[:, SMALL_BETA + h:SMALL_BETA + h + 1]
        lane = SMALL_DECAY + h
        g_col = g_all[:, lane:lane + 1]
        gc_col = g_cum[:, lane:lane + 1]
        gc_row = _mm_01x(ones8, g_col * upper01.astype(F32), parts=3)[0:1, :]
        decay = jnp.exp(jnp.where(lower, gc_col - gc_row, -jnp.inf))
        kb = k * beta
        kbh, kh, qh = kb.astype(BF16), k.astype(BF16), q.astype(BF16)
        a_mat = jnp.where(strict, _mm(kbh, kh, _NT) * decay, 0.0)
        inv = _unit_lower_inverse(a_mat, tb)
        rhs = jnp.concatenate([v * beta, kb * e_cum[:, lane:lane + 1]], axis=1)
        sol = _mm3(inv, rhs)
        u_base, w = sol[:, 0:DV_B], sol[:, DV_B:]
        qk = _mm(qh, kh, _NT) * decay
        q_dec = (q * e_cum[:, lane:lane + 1]).astype(BF16)
        k_dec = (k * e_rest[:, lane:lane + 1]).astype(BF16)
        wh = w.astype(BF16)
        qkh = qk.astype(BF16)
        state = s_scr[h]
        outs = []
        for n in range(tb // DN_CHUNK):
            lo, hi = n * DN_CHUNK, (n + 1) * DN_CHUNK
            sh = state.astype(BF16)
            u = u_base[lo:hi] - _mm(wh[lo:hi], sh)
            uh = u.astype(BF16)
            outs.append(_mm(q_dec[lo:hi], sh) + _mm(qkh[lo:hi, lo:hi], uh))
            state = state * e_tot[lo:lo + 1, lane:lane + 1] + _mm(k_dec[lo:hi], uh, _TN)
        s_scr[h] = state
        o = jnp.concatenate(outs, axis=0)
        o = o * lax.rsqrt(jnp.mean(o * o, axis=-1, keepdims=True) + EPS) * gn_ref[...]
        o_ref[:, h * DV_B:(h + 1) * DV_B] = (o * _silu(z_ref[:, h * DV_B:(h + 1) * DV_B])).astype(BF16)

    @pl.when(ci == pl.num_programs(1) - 1)
    def _():
        s_ref[...] = s_scr[...]


def _dn_prompt(braw, bz, small, cw, par, gn, batch, seq):
    tb = DN_BLOCK
    nb = seq // tb
    row = lambda b, i: (b * nb + i, 0)
    const = lambda b, i: (0, 0)
    return pl.pallas_call(
        _dn_prompt_kernel,
        grid=(batch, nb),
        in_specs=[pl.BlockSpec((tb, CONV_CH), row), pl.BlockSpec((tb, W_B), row), pl.BlockSpec((tb, LANES), row),
                  pl.BlockSpec((CONV_W, CONV_CH), const), pl.BlockSpec((2, LANES), const),
                  pl.BlockSpec((1, DV_B), const)],
        out_specs=[pl.BlockSpec((tb, W_B), row),
                   pl.BlockSpec((None, H_B, DK_B, DV_B), lambda b, i: (b, 0, 0, 0))],
        out_shape=[jax.ShapeDtypeStruct((batch * seq, W_B), BF16),
                   jax.ShapeDtypeStruct((batch, H_B, DK_B, DV_B), F32)],
        scratch_shapes=[pltpu.VMEM((SUBLANES + tb, CONV_CH), F32), pltpu.VMEM((H_B, DK_B, DV_B), F32)],
        compiler_params=_cparams(("parallel", "arbitrary")),
        name="dn_prompt",
    )(braw, bz, small, cw, par, gn)


DN_DEC_SEQS = 8


def _dn_decode_kernel(raw_ref, z_ref, small_ref, conv_ref, st_ref, cw_ref, par_ref, gn_ref, o_ref, s_ref):
    nk = H_B * DK_B
    conv = raw_ref[...] * cw_ref[CONV_W - 1:CONV_W, :]
    for i in range(CONV_W - 1):
        conv = conv + conv_ref[:, i, :] * cw_ref[i:i + 1, :]
    qkv = _silu(conv)
    small = small_ref[...]
    beta_all = _sigmoid(small)
    eg_all = jnp.exp(-jnp.exp(par_ref[0:1, :]) * _softplus(small + par_ref[1:2, :]))
    zero7 = jnp.zeros((SUBLANES - 1, DK_B), F32)
    for h in range(H_B):
        q = _l2n(qkv[:, h * DK_B:(h + 1) * DK_B]) * (DK_B ** -0.5)
        k = _l2n(qkv[:, nk + h * DK_B:nk + (h + 1) * DK_B])
        v = qkv[:, 2 * nk + h * DV_B:2 * nk + (h + 1) * DV_B]
        beta = beta_all[:, SMALL_BETA + h:SMALL_BETA + h + 1]
        eg = eg_all[:, SMALL_DECAY + h:SMALL_DECAY + h + 1]
        qk = jnp.sum(q * k, axis=-1, keepdims=True)
        outs = []
        for s in range(DN_DEC_SEQS):
            state = st_ref[s, h]
            kq = jnp.concatenate([k[s:s + 1], q[s:s + 1], zero7[0:SUBLANES - 2]], axis=0).astype(BF16)
            proj = _mm(kq, state.astype(BF16))
            e = eg[s:s + 1]
            u = beta[s:s + 1] * (v[s:s + 1] - e * proj[0:1])
            outs.append(e * proj[1:2] + qk[s:s + 1] * u)
            k8 = jnp.concatenate([k[s:s + 1], zero7], axis=0).astype(BF16)
            u8 = jnp.concatenate([u, zero7], axis=0).astype(BF16)
            s_ref[s, h] = state * e + _mm(k8, u8, _TN)
        o = jnp.concatenate(outs, axis=0)
        o = o * lax.rsqrt(jnp.mean(o * o, axis=-1, keepdims=True) + EPS) * gn_ref[...]
        o_ref[:, h * DV_B:(h + 1) * DV_B] = (o * _silu(z_ref[:, h * DV_B:(h + 1) * DV_B])).astype(BF16)


def _dn_decode(braw, bz, small, state_conv, state_delta, cw, par, gn, layer):
    n_seq = braw.shape[0]
    ts = DN_DEC_SEQS
    row = lambda i: (i, 0)
    const = lambda i: (0, 0)
    return pl.pallas_call(
        _dn_decode_kernel,
        grid=(n_seq // ts,),
        in_specs=[pl.BlockSpec((ts, CONV_CH), row), pl.BlockSpec((ts, W_B), row), pl.BlockSpec((ts, LANES), row),
                  pl.BlockSpec((ts, None, CONV_W - 1, CONV_CH), lambda i: (i, layer, 0, 0)),
                  pl.BlockSpec((ts, None, H_B, DK_B, DV_B), lambda i: (i, layer, 0, 0, 0)),
                  pl.BlockSpec((CONV_W, CONV_CH), const), pl.BlockSpec((2, LANES), const),
                  pl.BlockSpec((1, DV_B), const)],
        out_specs=[pl.BlockSpec((ts, W_B), row),
                   pl.BlockSpec((ts, H_B, DK_B, DV_B), lambda i: (i, 0, 0, 0))],
        out_shape=[jax.ShapeDtypeStruct((n_seq, W_B), BF16),
                   jax.ShapeDtypeStruct((n_seq, H_B, DK_B, DV_B), F32)],
        compiler_params=_cparams(("parallel",)),
        name="dn_decode",
    )(braw, bz, small, state_conv, state_delta, cw, par, gn)


def _block_summaries(rows, kn0):
    n = rows.shape[0]
    m = jnp.mean(rows.reshape(n // CMP_BLOCK, CMP_BLOCK, LANES), axis=1)
    lane = lax.broadcasted_iota(jnp.int32, m.shape, 1)
    is_k = lane < DH_C
    ms = jnp.sum(jnp.where(is_k, m * m, 0.0), axis=-1, keepdims=True) * (1.0 / DH_C)
    return jnp.where(is_k, m * lax.rsqrt(ms + EPS) * kn0, m)


def _nsa_cmp_kernel(rows_ref, kn0_ref, o_ref):
    o_ref[...] = _block_summaries(rows_ref[...], kn0_ref[...]).astype(BF16)


def _nsa_cmp(nsa_rows, kn0, batch, seq):
    nb = seq // CMP_BLOCK
    return pl.pallas_call(
        _nsa_cmp_kernel,
        grid=(batch,),
        in_specs=[pl.BlockSpec((seq, LANES), lambda b: (b, 0)), pl.BlockSpec((1, LANES), lambda b: (0, 0))],
        out_specs=pl.BlockSpec((nb, LANES), lambda b: (b, 0)),
        out_shape=jax.ShapeDtypeStruct((batch * nb, LANES), BF16),
        compiler_params=_cparams(("parallel",)),
        name="nsa_cmp",
    )(nsa_rows, kn0)


def _alibi_slope_rows(n_rows, rows_per_head):
    h = lax.broadcasted_iota(jnp.int32, (n_rows, 1), 0) // rows_per_head
    return jnp.exp((h + 1).astype(F32) * (-8.0 / H_C * 0.6931471805599453))


def _masked_softmax(s, mask):
    s = jnp.where(mask, s, NEG_INF)
    m = jnp.max(s, axis=-1, keepdims=True)
    e = jnp.where(mask, jnp.exp(s - m), 0.0)
    return e / jnp.maximum(jnp.sum(e, axis=-1, keepdims=True), 1e-30)


def _topk_membership(score, n_blocks):
    j = lax.broadcasted_iota(jnp.int32, score.shape, 1)
    rank = jnp.zeros(score.shape, F32)
    for i in range(n_blocks):
        col = score[:, i:i + 1]
        ahead = (col > score) | ((col == score) & (j > i))
        rank = rank + ahead.astype(F32)
    return rank < float(min(TOP_K_BLOCKS, n_blocks))


def _stack_heads(x, width):
    return jnp.concatenate([x[:, h * width:(h + 1) * width] for h in range(H_C)], axis=0)


def _nsa_prompt_kernel(q_ref, small_ref, cmp_ref, kv_ref, gn_ref, o_ref, *, n_blocks):
    i = pl.program_id(1)
    nr = H_C * TQ
    qs = _stack_heads(q_ref[...], DH_C)
    slope = _alibi_slope_rows(nr, TQ)
    t_row = i * TQ + lax.broadcasted_iota(jnp.int32, (nr, 1), 0) % TQ
    t_f = t_row.astype(F32)

    kcb = cmp_ref[:, 0:DH_C]
    vcb = cmp_ref[:, DH_C:2 * DH_C]
    blk = lax.broadcasted_iota(jnp.int32, (1, n_blocks), 1)
    blk_mid = (blk * CMP_BLOCK).astype(F32) + 0.5 * (CMP_BLOCK - 1)
    vis_c = ((blk + 1) * CMP_BLOCK - 1) <= t_row
    s_c = _mm(qs, kcb, _NT) - slope * (t_f - blk_mid)
    p_c = _masked_softmax(s_c, vis_c)
    o_c = _mm(p_c.astype(BF16), vcb)

    imp = p_c[0:TQ]
    for h in range(1, H_C):
        imp = imp + p_c[h * TQ:(h + 1) * TQ]
    t128 = t_row[0:TQ]
    cur = t128 // SEL_BLOCK
    jb = lax.broadcasted_iota(jnp.int32, (TQ, n_blocks), 1)
    forced = (jb == 0) | (jb == cur) | (jb == cur - 1)
    score = jnp.where(jb > cur, -1.0, jnp.where(forced, 2.0 * H_C, imp))
    sel = (_topk_membership(score, n_blocks) & (jb <= cur)).astype(BF16)
    sel4 = jnp.concatenate([sel] * H_C, axis=0)

    bpt = TQ // SEL_BLOCK
    er = lax.broadcasted_iota(jnp.int32, (n_blocks, TQ), 0)
    ec = lax.broadcasted_iota(jnp.int32, (n_blocks, TQ), 1) // SEL_BLOCK
    s_off = lax.broadcasted_iota(jnp.int32, (1, TQ), 1)

    def flash(j, carry, k_lo, v_lo, use_sel):
        m, l, acc = carry
        start = pl.multiple_of(j * TQ, TQ)
        k = kv_ref[pl.ds(start, TQ), k_lo:k_lo + DH_C]
        v = kv_ref[pl.ds(start, TQ), v_lo:v_lo + DH_C]
        s_pos = j * TQ + s_off
        s = _mm(qs, k, _NT) - slope * (t_f - s_pos.astype(F32))
        valid = s_pos <= t_row
        if use_sel:
            picked = _mm(sel4, (er == ec + j * bpt).astype(BF16))
            valid = valid & (picked > 0.5)
        else:
            valid = valid & (t_row - s_pos < WINDOW)
        s = jnp.where(valid, s, NEG_INF)
        m_new = jnp.maximum(m, jnp.max(s, axis=-1, keepdims=True))
        alpha = jnp.exp(m - m_new)
        p = jnp.where(valid, jnp.exp(s - m_new), 0.0)
        l = alpha * l + jnp.sum(p, axis=-1, keepdims=True)
        acc = alpha * acc + _mm(p.astype(BF16), v)
        return m_new, l, acc

    init = (jnp.full((nr, 1), NEG_INF, F32), jnp.zeros((nr, 1), F32), jnp.zeros((nr, DH_C), F32))
    _, l_s, acc_s = lax.fori_loop(0, i + 1, lambda j, cr: flash(j, cr, 0, DH_C, True), init)
    o_s = acc_s / jnp.maximum(l_s, 1e-30)
    first_w = jnp.maximum(i - WINDOW // TQ, 0)
    _, l_w, acc_w = lax.fori_loop(first_w, i + 1, lambda j, cr: flash(j, cr, 2 * DH_C, 3 * DH_C, False), init)
    o_w = acc_w / jnp.maximum(l_w, 1e-30)

    gates = _sigmoid(small_ref[...])
    def gate_col(rr):
        return jnp.concatenate([gates[:, SMALL_GATE + 3 * h + rr:SMALL_GATE + 3 * h + rr + 1]
                                for h in range(H_C)], axis=0)
    o = gate_col(0) * o_c + gate_col(1) * o_s + gate_col(2) * o_w
    o = o * lax.rsqrt(jnp.mean(o * o, axis=-1, keepdims=True) + EPS)
    o = jnp.concatenate([o[h * TQ:(h + 1) * TQ] for h in range(H_C)], axis=1)
    o_ref[...] = (o * gn_ref[...]).astype(BF16)


def _nsa_prompt(cq, small, cmp, ckv, gn, batch, seq):
    nq = seq // TQ
    nb = seq // CMP_BLOCK
    row = lambda b, i: (b * nq + i, 0)
    kern = functools.partial(_nsa_prompt_kernel, n_blocks=nb)
    return pl.pallas_call(
        kern,
        grid=(batch, nq),
        in_specs=[pl.BlockSpec((TQ, W_C), row), pl.BlockSpec((TQ, LANES), row),
                  pl.BlockSpec((nb, LANES), lambda b, i: (b, 0)),
                  pl.BlockSpec((seq, 4 * DH_C), lambda b, i: (b, 0)),
                  pl.BlockSpec((1, W_C), lambda b, i: (0, 0))],
        out_specs=pl.BlockSpec((TQ, W_C), row),
        out_shape=jax.ShapeDtypeStruct((batch * seq, W_C), BF16),
        compiler_params=_cparams(("parallel", "arbitrary")),
        name="nsa_prompt",
    )(cq, small, cmp, ckv, gn)


def _nsa_decode_kernel(pt_ref, q_ref, small_ref, cur_nsa_ref, cur_win_ref, win_ref, kn0_ref, gn_ref, cache_ref,
                       o_ref, buf_ref, sem_ref, *, layer, n_pages, page):
    s = pl.program_id(0)
    ns = pl.num_programs(0)
    slot = s % 2

    @pl.when(s == 0)
    def _():
        for cp in _page_copies(pt_ref, cache_ref, buf_ref, sem_ref, layer, 0, 0, n_pages, page):
            cp.start()

    @pl.when(s + 1 < ns)
    def _():
        for cp in _page_copies(pt_ref, cache_ref, buf_ref, sem_ref, layer, s + 1, 1 - slot, n_pages, page):
            cp.start()

    for cp in _page_copies(pt_ref, cache_ref, buf_ref, sem_ref, layer, s, slot, n_pages, page):
        cp.wait()

    n_past = n_pages * page
    n_cmp = n_past // CMP_BLOCK
    cur = n_past // SEL_BLOCK
    t_f = float(n_past)
    hr = SUBLANES

    qrow = q_ref[pl.ds(s, 1), :]
    q8 = jnp.concatenate([qrow[:, h * DH_C:(h + 1) * DH_C] for h in range(H_C)]
                         + [jnp.zeros((hr - H_C, DH_C), F32)], axis=0).astype(BF16)
    slope = _alibi_slope_rows(hr, 1)

    summ = _block_summaries(buf_ref[slot, :, 0:2 * DH_C], kn0_ref[...])
    kcb = summ[:, 0:DH_C].astype(BF16)
    vcb = summ[:, DH_C:2 * DH_C].astype(BF16)
    blk = lax.broadcasted_iota(jnp.int32, (1, n_cmp), 1)
    blk_mid = (blk * CMP_BLOCK).astype(F32) + 0.5 * (CMP_BLOCK - 1)
    s_c = _mm(q8, kcb, _NT) - slope * (t_f - blk_mid)
    p_c = _masked_softmax(s_c, blk >= 0)
    o_c = _mm(p_c.astype(BF16), vcb)

    hrow = lax.broadcasted_iota(jnp.int32, (hr, n_cmp), 0)
    imp = jnp.sum(jnp.where(hrow < H_C, p_c, 0.0), axis=0, keepdims=True)
    imp = jnp.concatenate([imp, jnp.zeros((1, LANES - n_cmp), F32)], axis=1)
    jb = lax.broadcasted_iota(jnp.int32, (1, LANES), 1)
    forced = (jb == 0) | (jb == cur) | (jb == cur - 1)
    score = jnp.where(jb > cur, -1.0, jnp.where(forced, 2.0 * H_C, imp))
    srow = jnp.broadcast_to(score, (LANES, LANES))
    scol = srow.T
    ri = lax.broadcasted_iota(jnp.int32, (LANES, LANES), 0)
    ci = lax.broadcasted_iota(jnp.int32, (LANES, LANES), 1)
    ahead = (scol > srow) | ((scol == srow) & (ri < ci))
    rank = jnp.sum(ahead.astype(F32), axis=0, keepdims=True)
    sel = ((rank < float(min(TOP_K_BLOCKS, cur + 1))) & (jb <= cur)).astype(BF16)
    sel8 = jnp.broadcast_to(sel, (hr, LANES))
    er = lax.broadcasted_iota(jnp.int32, (LANES, n_past), 0)
    ec = lax.broadcasted_iota(jnp.int32, (LANES, n_past), 1) // SEL_BLOCK
    picked = _mm(sel8, (er == ec).astype(BF16)) > 0.5

    ks = buf_ref[slot, :, 2 * DH_C:3 * DH_C].astype(BF16)
    vs = buf_ref[slot, :, 3 * DH_C:4 * DH_C].astype(BF16)
    pos = lax.broadcasted_iota(jnp.int32, (1, n_past), 1).astype(F32)
    s_s = _mm(q8, ks, _NT) - slope * (t_f - pos)
    cur_row = cur_nsa_ref[pl.ds(s, 1), :]
    q8f = q8.astype(F32)
    s_cur = jnp.sum(q8f * cur_row[:, 2 * DH_C:3 * DH_C].astype(BF16).astype(F32), axis=-1, keepdims=True)
    s_s = jnp.where(picked, s_s, NEG_INF)
    m = jnp.maximum(jnp.max(s_s, axis=-1, keepdims=True), s_cur)
    e = jnp.where(picked, jnp.exp(s_s - m), 0.0)
    e_cur = jnp.exp(s_cur - m)
    den = jnp.sum(e, axis=-1, keepdims=True) + e_cur
    o_s = (_mm(e.astype(BF16), vs) + e_cur * cur_row[:, 3 * DH_C:4 * DH_C]) / den

    n_buf = win_ref.shape[0]
    kw = win_ref[:, 0:DH_C].astype(BF16)
    vw = win_ref[:, DH_C:2 * DH_C].astype(BF16)
    wpos = n_past - n_buf + lax.broadcasted_iota(jnp.int32, (1, n_buf), 1)
    vis_w = (n_past - wpos < WINDOW) & (wpos >= 0)
    s_w = _mm(q8, kw, _NT) - slope * (t_f - wpos.astype(F32))
    cur_w = cur_win_ref[pl.ds(s, 1), :]
    sw_cur = jnp.sum(q8f * cur_w[:, 0:DH_C].astype(BF16).astype(F32), axis=-1, keepdims=True)
    s_w = jnp.where(vis_w, s_w, NEG_INF)
    m = jnp.maximum(jnp.max(s_w, axis=-1, keepdims=True), sw_cur)
    e = jnp.where(vis_w, jnp.exp(s_w - m), 0.0)
    e_cur = jnp.exp(sw_cur - m)
    den = jnp.sum(e, axis=-1, keepdims=True) + e_cur
    o_w = (_mm(e.astype(BF16), vw) + e_cur * cur_w[:, DH_C:2 * DH_C]) / den

    gates = jnp.broadcast_to(_sigmoid(small_ref[pl.ds(s, 1), :]), (hr, LANES))
    gl = lax.broadcasted_iota(jnp.int32, (hr, LANES), 1)
    gh = lax.broadcasted_iota(jnp.int32, (hr, LANES), 0)

    def gate_col(rr):
        return jnp.sum(jnp.where(gl == SMALL_GATE + 3 * gh + rr, gates, 0.0), axis=-1, keepdims=True)

    o = gate_col(0) * o_c + gate_col(1) * o_s + gate_col(2) * o_w
    o = o * lax.rsqrt(jnp.mean(o * o, axis=-1, keepdims=True) + EPS)
    orow = jnp.concatenate([o[h:h + 1] for h in range(H_C)], axis=1)
    o_ref[pl.ds(s, 1), :] = orow * gn_ref[...]


def _nsa_decode(page_table, cq, small, cur_nsa, cur_win, state_win, kn0, gn, cache, layer):
    n_seq, n_pages = page_table.shape
    page = cache.shape[2]
    n_buf = state_win.shape[2]
    kern = functools.partial(_nsa_decode_kernel, layer=layer, n_pages=n_pages, page=page)
    whole = lambda shape: pl.BlockSpec(shape, lambda s, pt: (0,) * len(shape))
    return pl.pallas_call(
        kern,
        grid_spec=pltpu.PrefetchScalarGridSpec(
            num_scalar_prefetch=1,
            grid=(n_seq,),
            in_specs=[whole((n_seq, W_C)), whole((n_seq, LANES)), whole((n_seq, 4 * DH_C)), whole((n_seq, 2 * DH_C)),
                      pl.BlockSpec((None, None, n_buf, 2 * DH_C), lambda s, pt: (s, layer, 0, 0)),
                      whole((1, LANES)), whole((1, W_C)),
                      pl.BlockSpec(memory_space=pl.ANY)],
            out_specs=whole((n_seq, W_C)),
            scratch_shapes=[pltpu.VMEM((2, n_pages * page, 4 * DH_C), F32),
                            pltpu.SemaphoreType.DMA((2,))]),
        out_shape=jax.ShapeDtypeStruct((n_seq, W_C), F32),
        compiler_params=_cparams(("arbitrary",)),
        name="nsa_decode",
    )(page_table, cq, small, cur_nsa, cur_win, state_win, kn0, gn, cache)


def _pack_w_in(w_in_l):
    widths = [W_A, W_A, W_A, H_B * DK_B, H_B * DK_B, W_B, W_B, H_B, H_B, W_C] + [DH_C] * 6 + [3 * H_C]
    cols, start = [], 0
    for w in widths:
        cols.append(w_in_l[:, start:start + w])
        start += w
    (a_q, a_k, a_v, b_q, b_k, b_v, b_z, b_beta, b_decay, c_q, c_kc, c_vc, c_ks, c_vs, c_kw, c_vw, c_gate) = cols
    pad = jnp.zeros((w_in_l.shape[0], LANES - 2 * H_B - 3 * H_C), w_in_l.dtype)
    packed = jnp.concatenate([a_k, a_v, a_q, b_q, b_k, b_v, b_z, c_q, c_kc, c_vc, c_ks, c_vs, c_kw, c_vw,
                              b_beta, b_decay, c_gate, pad], axis=1)
    return packed.astype(BF16)


def _row_tile(t, cap):
    tm = min(t, cap)
    while t % tm:
        tm //= 2
    return tm


def kernel(x_prompt, x_sample, cache_sb, cache_nsa, state_win, state_delta, state_conv, page_table, w_in, conv_w,
           a_log, dt_bias, b_out_norm, a_out_norm, c_q_norm, c_k_norm, c_out_norm, w_out, attn_norm, ffn_norm,
           dense_w_gu, dense_w_down, router_w, router_b, moe_w_gu, moe_w_down):
    batch, seq, _ = x_prompt.shape
    n_dec = x_sample.shape[0]
    depth = w_in.shape[0]
    n_phys, _, page = cache_sb.shape[:3]
    n_buf = state_win.shape[2]
    tp, ts = batch * seq, n_dec
    tm_p, tm_s = _row_tile(tp, 512), _row_tile(ts, 512)

    cache_sb2 = cache_sb.reshape(n_phys, depth, page, 2 * W_A)
    cache_nsa2 = cache_nsa.reshape(n_phys, depth, page, 4 * DH_C)
    state_win2 = state_win.reshape(n_dec, depth, n_buf, 2 * DH_C)
    d_ff = dense_w_down.shape[1]
    tf = d_ff // 2 if (d_ff // 2) % LANES == 0 else d_ff

    xp = x_prompt.reshape(tp, D_MODEL)
    xs = x_sample.reshape(ts, D_MODEL)
    outs_p = {k: [] for k in ("sb", "nsa", "win", "delta", "conv")}
    outs_s = {k: [] for k in ("sb", "nsa", "win", "delta", "conv")}
    for l in range(depth):
        w_pk = _pack_w_in(w_in[l])
        wo = w_out[l].astype(BF16)
        g_attn = attn_norm[l].reshape(1, D_MODEL)
        g_ffn = ffn_norm[l].reshape(1, D_MODEL)
        qn = jnp.tile(c_q_norm[l], H_C).reshape(1, W_C)
        ones64 = jnp.ones((DH_C,), F32)
        kn = jnp.concatenate([ones64, ones64, c_k_norm[l, 1], ones64, c_k_norm[l, 2], ones64]).reshape(1, 384)
        kn0 = jnp.concatenate([c_k_norm[l, 0], ones64]).reshape(1, LANES)
        par = (jnp.zeros((2, LANES), F32).at[0, SMALL_DECAY:SMALL_DECAY + H_B].set(a_log[l])
               .at[1, SMALL_DECAY:SMALL_DECAY + H_B].set(dt_bias[l]))
        gn_a = a_out_norm[l].reshape(1, W_A)
        gn_b = b_out_norm[l].reshape(1, DV_B)
        gn_c = c_out_norm[l].reshape(1, W_C)
        cw = conv_w[l]

        sb, aq, akv, braw, bz, small, cq, nsa, win, ckv = _inproj(xp, g_attn, w_pk, qn, kn, tm_p)
        oa = _sb_prompt(aq, akv, gn_a, batch, seq)
        ob, s_fin = _dn_prompt(braw, bz, small, cw, par, gn_b, batch, seq)
        cmp = _nsa_cmp(nsa, kn0, batch, seq)
        oc = _nsa_prompt(cq, small, cmp, ckv, gn_c, batch, seq)
        outs_p["sb"].append(sb.reshape(batch, seq, 2, H_A, DH_A))
        outs_p["nsa"].append(nsa.reshape(batch, seq, 4, DH_C))
        outs_p["win"].append(win.reshape(batch, seq, 2, DH_C)[:, seq - min(WINDOW, seq):])
        outs_p["delta"].append(s_fin)
        outs_p["conv"].append(braw.reshape(batch, seq, CONV_CH)[:, seq - (CONV_W - 1):])

        sb_s, aq_s, _, braw_s, bz_s, small_s, cq_s, nsa_s, win_s, _ = _inproj(xs, g_attn, w_pk, qn, kn, tm_s)
        oa_s = _sb_decode(page_table, aq_s.astype(F32), gn_a, cache_sb2, l).astype(BF16)
        ob_s, s_new = _dn_decode(braw_s, bz_s, small_s, state_conv, state_delta, cw, par, gn_b, l)
        oc_s = _nsa_decode(page_table, cq_s.astype(F32), small_s, nsa_s, win_s, state_win2, kn0, gn_c,
                           cache_nsa2, l).astype(BF16)
        outs_s["sb"].append(sb_s.reshape(n_dec, 1, 2, H_A, DH_A))
        outs_s["nsa"].append(nsa_s.reshape(n_dec, 1, 4, DH_C))
        win_all = jnp.concatenate([state_win[:, l], win_s.reshape(n_dec, 1, 2, DH_C)], axis=1)
        outs_s["win"].append(win_all[:, win_all.shape[1] - n_buf:])
        outs_s["delta"].append(s_new)
        outs_s["conv"].append(jnp.concatenate([state_conv[:, l], braw_s.reshape(n_dec, 1, CONV_CH)], axis=1)[:, 1:])

        if l % 2 == 0:
            wgu = dense_w_gu[l // 2].astype(BF16)
            wd = dense_w_down[l // 2].astype(BF16)
            xp = _ffn_dense(xp, oa, ob, oc, wo, g_ffn, wgu, wd, tm_p, tf)
            xs = _ffn_dense(xs, oa_s, ob_s, oc_s, wo, g_ffn, wgu, wd, tm_s, tf)
        else:
            wr = jnp.pad(router_w[l // 2], ((0, 0), (0, LANES - N_EXPERTS)))
            br = jnp.pad(router_b[l // 2], (0, LANES - N_EXPERTS)).reshape(1, LANES)
            wgu = moe_w_gu[l // 2].astype(BF16)
            wd = moe_w_down[l // 2].astype(BF16)
            xp = _ffn_moe(xp, oa, ob, oc, wo, g_ffn, wr, br, wgu, wd, tm_p)
            xs = _ffn_moe(xs, oa_s, ob_s, oc_s, wo, g_ffn, wr, br, wgu, wd, tm_s)

    stack = lambda xs_: jnp.stack(xs_, axis=1)
    return (xp.reshape(batch, seq, D_MODEL), xs.reshape(n_dec, 1, D_MODEL),
            stack(outs_p["sb"]), stack(outs_s["sb"]), stack(outs_p["nsa"]), stack(outs_s["nsa"]),
            stack(outs_p["win"]), stack(outs_s["win"]), stack(outs_p["delta"]), stack(outs_s["delta"]),
            stack(outs_p["conv"]), stack(outs_s["conv"]))
```

```python
import functools

import jax
import jax.numpy as jnp
from jax import lax
from jax.experimental import pallas as pl
from jax.experimental.pallas import tpu as pltpu

F32 = jnp.float32
BF16 = jnp.bfloat16

D_MODEL = 1024
H_A, DH_A = 4, 64
H_B, DK_B, DV_B = 4, 128, 128
H_C, DH_C = 4, 64
W_A = H_A * DH_A
W_B = H_B * DV_B
W_C = H_C * DH_C
CONV_W = 4
CONV_CH = 2 * H_B * DK_B + W_B
DN_CHUNK = 64
CMP_BLOCK = 64
SEL_BLOCK = 64
TOP_K_BLOCKS = 16
WINDOW = 512
N_EXPERTS = 8
EPS = 1e-6
NEG_INF = -1e30

LANES = 128
SUBLANES = 8
VMEM_LIMIT = 56 * 1024 * 1024

N_PROJ = 512 + 256 + CONV_CH + W_B + W_C + 256 + 128 + 128
SMALL_BETA, SMALL_DECAY, SMALL_GATE = 0, H_B, 2 * H_B
TQ = 128
SB_TK = 256
NSA_TK = 256
LOG2E = 1.4426950408889634
DN_BLOCK = 256


def _cparams(sem):
    return pltpu.CompilerParams(dimension_semantics=sem, vmem_limit_bytes=VMEM_LIMIT)


def _split2(x):
    hi = x.astype(BF16)
    lo = (x - hi.astype(F32)).astype(BF16)
    return hi, lo


def _split3(x):
    hi = x.astype(BF16)
    r = x - hi.astype(F32)
    mid = r.astype(BF16)
    lo = (r - mid.astype(F32)).astype(BF16)
    return hi, mid, lo


_NN = (((1,), (0,)), ((), ()))
_NT = (((1,), (1,)), ((), ()))
_TN = (((0,), (0,)), ((), ()))


def _mm(a, b, dims=_NN):
    return lax.dot_general(a, b, dims, preferred_element_type=F32)


def _mm_x01(x, m01, parts=2):
    ps = _split2(x) if parts == 2 else _split3(x)
    out = _mm(ps[0], m01)
    for p in ps[1:]:
        out = out + _mm(p, m01)
    return out


def _mm_01x(m01, x, parts=2):
    ps = _split2(x) if parts == 2 else _split3(x)
    out = _mm(m01, ps[0])
    for p in ps[1:]:
        out = out + _mm(m01, p)
    return out


def _mm3(a, b, dims=_NN):
    ah, al = _split2(a)
    bh, bl = _split2(b)
    return _mm(ah, bh, dims) + (_mm(ah, bl, dims) + _mm(al, bh, dims))


def _silu(x):
    return x * (1.0 / (1.0 + jnp.exp(-x)))


def _sigmoid(x):
    return 1.0 / (1.0 + jnp.exp(-x))


def _softplus(x):
    return jnp.maximum(x, 0.0) + jnp.log(1.0 + jnp.exp(-jnp.abs(x)))


def _group_ones(n, group):
    r = lax.broadcasted_iota(jnp.int32, (n, n), 0) // group
    c = lax.broadcasted_iota(jnp.int32, (n, n), 1) // group
    return (r == c).astype(BF16)


def _inproj_kernel(x_ref, g_ref, w_ref, qn_ref, kn_ref,
                   sb_ref, aq_ref, akv_ref, braw_ref, bz_ref, small_ref, cq_ref, nsa_ref, win_ref, ckv_ref):
    x = x_ref[...]
    h = x * lax.rsqrt(jnp.mean(x * x, axis=-1, keepdims=True) + EPS) * g_ref[...]
    p = _mm(h.astype(BF16), w_ref[...], _NT)
    o = 0
    sb = p[:, o:o + 512]; o += 512
    aq = p[:, o:o + 256]; o += 256
    braw = p[:, o:o + CONV_CH]; o += CONV_CH
    bz = p[:, o:o + W_B]; o += W_B
    cq = p[:, o:o + W_C]; o += W_C
    nsa = p[:, o:o + 256]; o += 256
    win = p[:, o:o + 128]; o += 128
    small = p[:, o:o + 128]
    sb_ref[...] = sb
    aq_ref[...] = (aq * (DH_A ** -0.5 * LOG2E)).astype(BF16)
    akv_ref[...] = sb.astype(BF16)
    braw_ref[...] = braw
    bz_ref[...] = bz
    small_ref[...] = small
    g64 = _group_ones(256, DH_C)
    ms = _mm_x01(cq * cq, g64) * (1.0 / DH_C)
    cqn = cq * lax.rsqrt(ms + EPS) * qn_ref[...]
    cq_ref[...] = (cqn * (DH_C ** -0.5 * LOG2E)).astype(BF16)
    lane = lax.broadcasted_iota(jnp.int32, nsa.shape, 1)
    is_ks = (lane >= 128) & (lane < 192)
    ms = _mm_x01(nsa * nsa, g64) * (1.0 / DH_C)
    nsa_n = jnp.where(is_ks, nsa * lax.rsqrt(ms + EPS) * kn_ref[:, 0:256], nsa)
    nsa_ref[...] = nsa_n
    lane = lax.broadcasted_iota(jnp.int32, win.shape, 1)
    ms = _mm_x01(win * win, g64[0:128, 0:128]) * (1.0 / DH_C)
    win_n = jnp.where(lane < 64, win * lax.rsqrt(ms + EPS) * kn_ref[:, 256:384], win)
    win_ref[...] = win_n
    ckv_ref[...] = jnp.concatenate([nsa_n[:, 128:256], win_n], axis=1).astype(BF16)


def _inproj(x, g, w, qn, kn, tm):
    t = x.shape[0]
    row = lambda i: (i, 0)
    const = lambda i: (0, 0)
    outs = [(512, F32), (256, BF16), (512, BF16), (CONV_CH, F32), (W_B, F32), (128, F32),
            (W_C, BF16), (256, F32), (128, F32), (256, BF16)]
    return pl.pallas_call(
        _inproj_kernel,
        grid=(t // tm,),
        in_specs=[pl.BlockSpec((tm, D_MODEL), row), pl.BlockSpec((1, D_MODEL), const),
                  pl.BlockSpec((N_PROJ, D_MODEL), const), pl.BlockSpec((1, 256), const),
                  pl.BlockSpec((1, 384), const)],
        out_specs=[pl.BlockSpec((tm, n), row) for n, _ in outs],
        out_shape=[jax.ShapeDtypeStruct((t, n), d) for n, d in outs],
        compiler_params=_cparams(("parallel",)),
        name="inproj",
    )(x, g, w, qn, kn)


def _mix_prologue(x_ref, oa_ref, ob_ref, oc_ref, wo_ref, g_ref, xn_scr, h_scr):
    mix = (_mm(oa_ref[...], wo_ref[0:W_A, :]) + _mm(ob_ref[...], wo_ref[W_A:W_A + W_B, :])
           + _mm(oc_ref[...], wo_ref[W_A + W_B:, :]))
    xn = x_ref[...] + mix
    xn_scr[...] = xn
    h = xn * lax.rsqrt(jnp.mean(xn * xn, axis=-1, keepdims=True) + EPS) * g_ref[...]
    h_scr[...] = h.astype(BF16)
    return h


def _ffn_dense_kernel(x_ref, oa_ref, ob_ref, oc_ref, wo_ref, g_ref, wg_ref, wu_ref, wd_ref, o_ref,
                      xn_scr, h_scr, acc_scr):
    f = pl.program_id(1)

    @pl.when(f == 0)
    def _():
        _mix_prologue(x_ref, oa_ref, ob_ref, oc_ref, wo_ref, g_ref, xn_scr, h_scr)
        acc_scr[...] = jnp.zeros_like(acc_scr)

    h = h_scr[...]
    act = _silu(_mm(h, wg_ref[...])) * _mm(h, wu_ref[...])
    acc_scr[...] += _mm(act.astype(BF16), wd_ref[...])

    @pl.when(f == pl.num_programs(1) - 1)
    def _():
        o_ref[...] = xn_scr[...] + acc_scr[...]


def _ffn_dense(x, oa, ob, oc, wo, g, wgu, wd, tm, tf):
    t = x.shape[0]
    d_ff = wd.shape[0]
    nf = d_ff // tf
    row = lambda i, f: (i, 0)
    const = lambda i, f: (0, 0)
    return pl.pallas_call(
        _ffn_dense_kernel,
        grid=(t // tm, nf),
        in_specs=[pl.BlockSpec((tm, D_MODEL), row), pl.BlockSpec((tm, W_A), row), pl.BlockSpec((tm, W_B), row),
                  pl.BlockSpec((tm, W_C), row), pl.BlockSpec((D_MODEL, D_MODEL), const),
                  pl.BlockSpec((1, D_MODEL), const),
                  pl.BlockSpec((D_MODEL, tf), lambda i, f: (0, f)),
                  pl.BlockSpec((D_MODEL, tf), lambda i, f: (0, f + nf)),
                  pl.BlockSpec((tf, D_MODEL), lambda i, f: (f, 0))],
        out_specs=pl.BlockSpec((tm, D_MODEL), row),
        out_shape=jax.ShapeDtypeStruct((t, D_MODEL), F32),
        scratch_shapes=[pltpu.VMEM((tm, D_MODEL), F32), pltpu.VMEM((tm, D_MODEL), BF16),
                        pltpu.VMEM((tm, D_MODEL), F32)],
        compiler_params=_cparams(("parallel", "arbitrary")),
        name="ffn_dense",
    )(x, oa, ob, oc, wo, g, wgu, wgu, wd)


def _ffn_moe_kernel(x_ref, oa_ref, ob_ref, oc_ref, wo_ref, g_ref, wr_ref, br_ref, wg_ref, wu_ref, wd_ref, o_ref,
                    xn_scr, h_scr, acc_scr, gate_scr):
    e = pl.program_id(1)

    @pl.when(e == 0)
    def _():
        h = _mix_prologue(x_ref, oa_ref, ob_ref, oc_ref, wo_ref, g_ref, xn_scr, h_scr)
        acc_scr[...] = jnp.zeros_like(acc_scr)
        logits = _mm3(h, wr_ref[...]) + br_ref[...]
        lane = lax.broadcasted_iota(jnp.int32, logits.shape, 1)
        logits = jnp.where(lane < N_EXPERTS, logits, -jnp.inf)
        m1 = jnp.max(logits, axis=-1, keepdims=True)
        i1 = jnp.min(jnp.where(logits == m1, lane, LANES), axis=-1, keepdims=True)
        rest = jnp.where(lane == i1, -jnp.inf, logits)
        m2 = jnp.max(rest, axis=-1, keepdims=True)
        i2 = jnp.min(jnp.where(rest == m2, lane, LANES), axis=-1, keepdims=True)
        e2 = jnp.exp(m2 - m1)
        den = 1.0 + e2
        gate_scr[...] = jnp.where(lane == i1, 1.0 / den, 0.0) + jnp.where(lane == i2, e2 / den, 0.0)

    h = h_scr[...]
    act = _silu(_mm(h, wg_ref[...])) * _mm(h, wu_ref[...])
    y = _mm(act.astype(BF16), wd_ref[...])
    lane = lax.broadcasted_iota(jnp.int32, gate_scr.shape, 1)
    ge = jnp.sum(jnp.where(lane == e, gate_scr[...], 0.0), axis=-1, keepdims=True)
    acc_scr[...] += ge * y

    @pl.when(e == pl.num_programs(1) - 1)
    def _():
        o_ref[...] = xn_scr[...] + acc_scr[...]


def _ffn_moe(x, oa, ob, oc, wo, g, wr, br, wgu, wd, tm):
    t = x.shape[0]
    ne, _, two_f = wgu.shape
    f = two_f // 2
    row = lambda i, e: (i, 0)
    const = lambda i, e: (0, 0)
    return pl.pallas_call(
        _ffn_moe_kernel,
        grid=(t // tm, ne),
        in_specs=[pl.BlockSpec((tm, D_MODEL), row), pl.BlockSpec((tm, W_A), row), pl.BlockSpec((tm, W_B), row),
                  pl.BlockSpec((tm, W_C), row), pl.BlockSpec((D_MODEL, D_MODEL), const),
                  pl.BlockSpec((1, D_MODEL), const), pl.BlockSpec((D_MODEL, LANES), const),
                  pl.BlockSpec((1, LANES), const),
                  pl.BlockSpec((None, D_MODEL, f), lambda i, e: (e, 0, 0)),
                  pl.BlockSpec((None, D_MODEL, f), lambda i, e: (e, 0, 1)),
                  pl.BlockSpec((None, f, D_MODEL), lambda i, e: (e, 0, 0))],
        out_specs=pl.BlockSpec((tm, D_MODEL), row),
        out_shape=jax.ShapeDtypeStruct((t, D_MODEL), F32),
        scratch_shapes=[pltpu.VMEM((tm, D_MODEL), F32), pltpu.VMEM((tm, D_MODEL), BF16),
                        pltpu.VMEM((tm, D_MODEL), F32), pltpu.VMEM((tm, LANES), F32)],
        compiler_params=_cparams(("parallel", "arbitrary")),
        name="ffn_moe",
    )(x, oa, ob, oc, wo, g, wr, br, wgu, wgu, wd)


def _sb_log2_terms(z2):
    sp = jnp.log2(1.0 + jnp.exp2(-jnp.abs(z2)))
    ls = jnp.minimum(z2, 0.0) - sp
    return ls, ls - z2


def _sb_prompt_kernel(q_ref, kv_ref, gn_ref, o_ref):
    i = pl.program_id(1)
    tk = SB_TK
    nr = H_A * TQ
    r = lax.broadcasted_iota(jnp.int32, (tk, tk), 0)
    c = lax.broadcasted_iota(jnp.int32, (tk, tk), 1)
    later01 = (r > c).astype(BF16)
    t_row = i * TQ + lax.broadcasted_iota(jnp.int32, (nr, 1), 0) % TQ
    s_off = lax.broadcasted_iota(jnp.int32, (1, tk), 1)
    qs = [q_ref[:, h * DH_A:(h + 1) * DH_A] for h in range(H_A)]

    def step(j, carry, diag):
        run, acc = carry
        start = pl.multiple_of(j * tk, tk)
        z2 = jnp.concatenate([_mm(qs[h], kv_ref[pl.ds(start, tk), h * DH_A:(h + 1) * DH_A], _NT)
                              for h in range(H_A)], axis=0)
        ls, lk = _sb_log2_terms(z2)
        if diag:
            before = (start + s_off) < t_row
            lk = jnp.where(before, lk, 0.0)
        suffix = _mm_x01(lk, later01)
        w = jnp.exp2(ls + suffix + run)
        if diag:
            w = jnp.where(before, w, 0.0)
        wh = w.astype(BF16)
        pv = jnp.concatenate([_mm(wh[h * TQ:(h + 1) * TQ],
                                  kv_ref[pl.ds(start, tk), W_A + h * DH_A:W_A + (h + 1) * DH_A])
                              for h in range(H_A)], axis=0)
        return run + jnp.sum(lk, axis=-1, keepdims=True), acc + pv

    jd = i // (tk // TQ)
    carry = (jnp.zeros((nr, 1), F32), jnp.zeros((nr, DH_A), F32))
    carry = step(jd, carry, True)
    carry = lax.fori_loop(0, jd, lambda jj, cr: step(jd - 1 - jj, cr, False), carry)
    o = carry[1]
    o = o * lax.rsqrt(jnp.mean(o * o, axis=-1, keepdims=True) + EPS)
    o = jnp.concatenate([o[h * TQ:(h + 1) * TQ] for h in range(H_A)], axis=1)
    o_ref[...] = (o * gn_ref[...]).astype(BF16)


def _sb_prompt(aq, akv, gn, batch, seq):
    nq = seq // TQ
    return pl.pallas_call(
        _sb_prompt_kernel,
        grid=(batch, nq),
        in_specs=[pl.BlockSpec((TQ, W_A), lambda b, i: (b * nq + i, 0)),
                  pl.BlockSpec((seq, 2 * W_A), lambda b, i: (b, 0)),
                  pl.BlockSpec((1, W_A), lambda b, i: (0, 0))],
        out_specs=pl.BlockSpec((TQ, W_A), lambda b, i: (b * nq + i, 0)),
        out_shape=jax.ShapeDtypeStruct((batch * seq, W_A), BF16),
        compiler_params=_cparams(("parallel", "arbitrary")),
        name="sb_prompt",
    )(aq, akv, gn)


def _page_copies(pt_ref, cache_ref, buf_ref, sem_ref, layer, seq_idx, slot, n_pages, page):
    return [pltpu.make_async_copy(cache_ref.at[pt_ref[seq_idx, p], layer],
                                  buf_ref.at[slot, :, pl.ds(p * page, page)], sem_ref.at[slot])
            for p in range(n_pages)]


def _chunk_rows(x, n_chunks):
    return jnp.concatenate([x[:, c * LANES:(c + 1) * LANES] for c in range(n_chunks)], axis=0)


def _chunk_lanes(x, n_chunks):
    return jnp.concatenate([x[c * SUBLANES:(c + 1) * SUBLANES] for c in range(n_chunks)], axis=1)


def _sb_decode_kernel(pt_ref, q_ref, gn_ref, cache_ref, o_ref, buf_ref, sem_ref, *, layer, n_pages, page):
    s = pl.program_id(0)
    ns = pl.num_programs(0)
    slot = s % 2

    @pl.when(s == 0)
    def _():
        for cp in _page_copies(pt_ref, cache_ref, buf_ref, sem_ref, layer, 0, 0, n_pages, page):
            cp.start()

    @pl.when(s + 1 < ns)
    def _():
        for cp in _page_copies(pt_ref, cache_ref, buf_ref, sem_ref, layer, s + 1, 1 - slot, n_pages, page):
            cp.start()

    for cp in _page_copies(pt_ref, cache_ref, buf_ref, sem_ref, layer, s, slot, n_pages, page):
        cp.wait()

    hr = SUBLANES
    nc = n_pages * page // LANES
    q = q_ref[pl.ds(s, 1), :]
    rr = lax.broadcasted_iota(jnp.int32, (hr, W_A), 0)
    cc = lax.broadcasted_iota(jnp.int32, (hr, W_A), 1)
    own = rr == cc // DH_A
    q8 = jnp.where(own, q, 0.0).astype(BF16)
    z2 = _mm(q8, buf_ref[slot, 0:W_A, :].astype(BF16))
    ls, lk = _sb_log2_terms(z2)
    ls_c, lk_c = _chunk_rows(ls, nc), _chunk_rows(lk, nc)
    r = lax.broadcasted_iota(jnp.int32, (LANES, LANES), 0)
    c = lax.broadcasted_iota(jnp.int32, (LANES, LANES), 1)
    inner = _mm_x01(lk_c, (r > c).astype(BF16))
    tot = jnp.broadcast_to(jnp.sum(lk_c, axis=-1, keepdims=True), (hr * nc, LANES))
    r = lax.broadcasted_iota(jnp.int32, (hr * nc, hr * nc), 0)
    c = lax.broadcasted_iota(jnp.int32, (hr * nc, hr * nc), 1)
    later_chunk01 = ((r % hr == c % hr) & (c // hr > r // hr)).astype(BF16)
    outer = _mm_01x(later_chunk01, tot)
    w = _chunk_lanes(jnp.exp2(ls_c + inner + outer), nc)
    o8 = _mm(w.astype(BF16), buf_ref[slot, W_A:2 * W_A, :].astype(BF16), _NT)
    o = jnp.sum(jnp.where(own, o8, 0.0), axis=0, keepdims=True)
    ms = _mm_x01(o * o, _group_ones(W_A, DH_A)) * (1.0 / DH_A)
    o_ref[pl.ds(s, 1), :] = o * lax.rsqrt(ms + EPS) * gn_ref[...]


def _sb_decode(page_table, aq, gn, cache, layer):
    n_seq, n_pages = page_table.shape
    page = cache.shape[3]
    kern = functools.partial(_sb_decode_kernel, layer=layer, n_pages=n_pages, page=page)
    return pl.pallas_call(
        kern,
        grid_spec=pltpu.PrefetchScalarGridSpec(
            num_scalar_prefetch=1,
            grid=(n_seq,),
            in_specs=[pl.BlockSpec((n_seq, W_A), lambda s, pt: (0, 0)),
                      pl.BlockSpec((1, W_A), lambda s, pt: (0, 0)),
                      pl.BlockSpec(memory_space=pl.ANY)],
            out_specs=pl.BlockSpec((n_seq, W_A), lambda s, pt: (0, 0)),
            scratch_shapes=[pltpu.VMEM((2, 2 * W_A, n_pages * page), F32),
                            pltpu.SemaphoreType.DMA((2,))]),
        out_shape=jax.ShapeDtypeStruct((n_seq, W_A), F32),
        compiler_params=_cparams(("arbitrary",)),
        name="sb_decode",
    )(page_table, aq, gn, cache)


def _unit_lower_inverse(a_mats, n):
    r = lax.broadcasted_iota(jnp.int32, (n, n), 0)
    c = lax.broadcasted_iota(jnp.int32, (n, n), 1)
    eye = (r == c).astype(F32)
    base = 16
    xs = [jnp.where(r // base == c // base, -a, 0.0) for a in a_mats]
    invs = [eye + x for x in xs]
    span = 1
    while 2 * span < base:
        xs = [_mm3(x, x) for x in xs]
        invs = [_mm3(inv, eye + x) for inv, x in zip(invs, xs)]
        span *= 2
    size = base
    while size < DN_CHUNK:
        off_mask = (r // (2 * size) == c // (2 * size)) & (r // size != c // size)
        mids = [_mm3(jnp.where(off_mask, a, 0.0), inv) for a, inv in zip(a_mats, invs)]
        invs = [inv - _mm3(inv, mid) for inv, mid in zip(invs, mids)]
        size *= 2
    return invs


def _l2n(x):
    return x * lax.rsqrt(jnp.sum(x * x, axis=-1, keepdims=True) + EPS)


def _dn_prompt_kernel(raw_ref, z_ref, small_ref, cw_ref, par_ref, gn_ref, o_ref, s_ref, xbuf, s_scr):
    ci = pl.program_id(1)
    tb = DN_BLOCK
    nk = H_B * DK_B

    @pl.when(ci == 0)
    def _():
        xbuf[0:SUBLANES, :] = jnp.zeros((SUBLANES, CONV_CH), F32)
        s_scr[...] = jnp.zeros_like(s_scr)

    xbuf[SUBLANES:SUBLANES + tb, :] = raw_ref[...]
    conv = xbuf[SUBLANES - 3:SUBLANES - 3 + tb, :] * cw_ref[0:1, :]
    for i in range(1, CONV_W):
        conv = conv + xbuf[SUBLANES - 3 + i:SUBLANES - 3 + i + tb, :] * cw_ref[i:i + 1, :]
    xbuf[0:SUBLANES, :] = xbuf[tb:tb + SUBLANES, :]
    qkv = _silu(conv)

    small = small_ref[...]
    beta_all = _sigmoid(small)
    g_all = -jnp.exp(par_ref[0:1, :]) * _softplus(small + par_ref[1:2, :])

    r = lax.broadcasted_iota(jnp.int32, (tb, tb), 0)
    c = lax.broadcasted_iota(jnp.int32, (tb, tb), 1)
    same = (r // DN_CHUNK) == (c // DN_CHUNK)
    lower = same & (r >= c)
    strict = same & (r > c)
    lower01 = lower.astype(BF16)
    upper01 = (same & (r <= c)).astype(BF16)
    same01 = same.astype(BF16)
    ones8 = jnp.ones((SUBLANES, tb), BF16)

    g_cum = _mm_01x(lower01, g_all, parts=3)
    g_tot = _mm_01x(same01, g_all, parts=3)
    e_cum = jnp.exp(g_cum)
    e_rest = jnp.exp(g_tot - g_cum)
    e_tot = jnp.exp(g_tot)

    heads = range(H_B)
    lanes = [SMALL_DECAY + h for h in heads]
    qs = [_l2n(qkv[:, h * DK_B:(h + 1) * DK_B]) * (DK_B ** -0.5) for h in heads]
    ks = [_l2n(qkv[:, nk + h * DK_B:nk + (h + 1) * DK_B]) for h in heads]
    vs = [qkv[:, 2 * nk + h * DV_B:2 * nk + (h + 1) * DV_B] for h in heads]
    betas = [beta_all[:, SMALL_BETA + h:SMALL_BETA + h + 1] for h in heads]
    upper_f = upper01.astype(F32)
    gc_rows = [_mm_01x(ones8, g_all[:, ln:ln + 1] * upper_f, parts=3)[0:1, :] for ln in lanes]
    decays = [jnp.exp(jnp.where(lower, g_cum[:, ln:ln + 1] - gr, -jnp.inf)) for ln, gr in zip(lanes, gc_rows)]
    kbs = [k * b for k, b in zip(ks, betas)]
    khs = [k.astype(BF16) for k in ks]
    qhs = [q.astype(BF16) for q in qs]
    a_mats = [jnp.where(strict, _mm(kb.astype(BF16), kh, _NT) * d, 0.0) for kb, kh, d in zip(kbs, khs, decays)]
    invs = _unit_lower_inverse(a_mats, tb)
    sols = [_mm3(inv, jnp.concatenate([v * b, kb * e_cum[:, ln:ln + 1]], axis=1))
            for inv, v, b, kb, ln in zip(invs, vs, betas, kbs, lanes)]
    u_bases = [sol[:, 0:DV_B] for sol in sols]
    whs = [sol[:, DV_B:].astype(BF16) for sol in sols]
    qkhs = [(_mm(qh, kh, _NT) * d).astype(BF16) for qh, kh, d in zip(qhs, khs, decays)]
    q_decs = [(q * e_cum[:, ln:ln + 1]).astype(BF16) for q, ln in zip(qs, lanes)]
    k_decs = [(k * e_rest[:, ln:ln + 1]).astype(BF16) for k, ln in zip(ks, lanes)]
    states = [s_scr[h] for h in heads]
    outs = [[] for _ in heads]
    for n in range(tb // DN_CHUNK):
        lo, hi = n * DN_CHUNK, (n + 1) * DN_CHUNK
        shs = [st.astype(BF16) for st in states]
        us = [ub[lo:hi] - _mm(wh[lo:hi], sh) for ub, wh, sh in zip(u_bases, whs, shs)]
        uhs = [u.astype(BF16) for u in us]
        for h in heads:
            outs[h].append(_mm(q_decs[h][lo:hi], shs[h]) + _mm(qkhs[h][lo:hi, lo:hi], uhs[h]))
        states = [st * e_tot[lo:lo + 1, ln:ln + 1] + _mm(kd[lo:hi], uh, _TN)
                  for st, ln, kd, uh in zip(states, lanes, k_decs, uhs)]
    for h in heads:
        s_scr[h] = states[h]
        o = jnp.concatenate(outs[h], axis=0)
        o = o * lax.rsqrt(jnp.mean(o * o, axis=-1, keepdims=True) + EPS) * gn_ref[...]
        o_ref[:, h * DV_B:(h + 1) * DV_B] = (o * _silu(z_ref[:, h * DV_B:(h + 1) * DV_B])).astype(BF16)

    @pl.when(ci == pl.num_programs(1) - 1)
    def _():
        s_ref[...] = s_scr[...]


def _dn_prompt(braw, bz, small, cw, par, gn, batch, seq):
    tb = DN_BLOCK
    nb = seq // tb
    row = lambda b, i: (b * nb + i, 0)
    const = lambda b, i: (0, 0)
    return pl.pallas_call(
        _dn_prompt_kernel,
        grid=(batch, nb),
        in_specs=[pl.BlockSpec((tb, CONV_CH), row), pl.BlockSpec((tb, W_B), row), pl.BlockSpec((tb, LANES), row),
                  pl.BlockSpec((CONV_W, CONV_CH), const), pl.BlockSpec((2, LANES), const),
                  pl.BlockSpec((1, DV_B), const)],
        out_specs=[pl.BlockSpec((tb, W_B), row),
                   pl.BlockSpec((None, H_B, DK_B, DV_B), lambda b, i: (b, 0, 0, 0))],
        out_shape=[jax.ShapeDtypeStruct((batch * seq, W_B), BF16),
                   jax.ShapeDtypeStruct((batch, H_B, DK_B, DV_B), F32)],
        scratch_shapes=[pltpu.VMEM((SUBLANES + tb, CONV_CH), F32), pltpu.VMEM((H_B, DK_B, DV_B), F32)],
        compiler_params=_cparams(("parallel", "arbitrary")),
        name="dn_prompt",
    )(braw, bz, small, cw, par, gn)


DN_DEC_SEQS = 8


def _dn_decode_kernel(raw_ref, z_ref, small_ref, conv_ref, st_ref, cw_ref, par_ref, gn_ref, o_ref, s_ref):
    nk = H_B * DK_B
    conv = raw_ref[...] * cw_ref[CONV_W - 1:CONV_W, :]
    for i in range(CONV_W - 1):
        conv = conv + conv_ref[:, i, :] * cw_ref[i:i + 1, :]
    qkv = _silu(conv)
    small = small_ref[...]
    beta_all = _sigmoid(small)
    eg_all = jnp.exp(-jnp.exp(par_ref[0:1, :]) * _softplus(small + par_ref[1:2, :]))
    zero7 = jnp.zeros((SUBLANES - 1, DK_B), F32)
    for h in range(H_B):
        q = _l2n(qkv[:, h * DK_B:(h + 1) * DK_B]) * (DK_B ** -0.5)
        k = _l2n(qkv[:, nk + h * DK_B:nk + (h + 1) * DK_B])
        v = qkv[:, 2 * nk + h * DV_B:2 * nk + (h + 1) * DV_B]
        beta = beta_all[:, SMALL_BETA + h:SMALL_BETA + h + 1]
        eg = eg_all[:, SMALL_DECAY + h:SMALL_DECAY + h + 1]
        qk = jnp.sum(q * k, axis=-1, keepdims=True)
        outs = []
        for s in range(DN_DEC_SEQS):
            state = st_ref[s, h]
            kq = jnp.concatenate([k[s:s + 1], q[s:s + 1], zero7[0:SUBLANES - 2]], axis=0).astype(BF16)
            proj = _mm(kq, state.astype(BF16))
            e = eg[s:s + 1]
            u = beta[s:s + 1] * (v[s:s + 1] - e * proj[0:1])
            outs.append(e * proj[1:2] + qk[s:s + 1] * u)
            k8 = jnp.concatenate([k[s:s + 1], zero7], axis=0).astype(BF16)
            u8 = jnp.concatenate([u, zero7], axis=0).astype(BF16)
            s_ref[s, h] = state * e + _mm(k8, u8, _TN)
        o = jnp.concatenate(outs, axis=0)
        o = o * lax.rsqrt(jnp.mean(o * o, axis=-1, keepdims=True) + EPS) * gn_ref[...]
        o_ref[:, h * DV_B:(h + 1) * DV_B] = (o * _silu(z_ref[:, h * DV_B:(h + 1) * DV_B])).astype(BF16)


def _dn_decode(braw, bz, small, state_conv, state_delta, cw, par, gn, layer):
    n_seq = braw.shape[0]
    ts = DN_DEC_SEQS
    row = lambda i: (i, 0)
    const = lambda i: (0, 0)
    return pl.pallas_call(
        _dn_decode_kernel,
        grid=(n_seq // ts,),
        in_specs=[pl.BlockSpec((ts, CONV_CH), row), pl.BlockSpec((ts, W_B), row), pl.BlockSpec((ts, LANES), row),
                  pl.BlockSpec((ts, None, CONV_W - 1, CONV_CH), lambda i: (i, layer, 0, 0)),
                  pl.BlockSpec((ts, None, H_B, DK_B, DV_B), lambda i: (i, layer, 0, 0, 0)),
                  pl.BlockSpec((CONV_W, CONV_CH), const), pl.BlockSpec((2, LANES), const),
                  pl.BlockSpec((1, DV_B), const)],
        out_specs=[pl.BlockSpec((ts, W_B), row),
                   pl.BlockSpec((ts, H_B, DK_B, DV_B), lambda i: (i, 0, 0, 0))],
        out_shape=[jax.ShapeDtypeStruct((n_seq, W_B), BF16),
                   jax.ShapeDtypeStruct((n_seq, H_B, DK_B, DV_B), F32)],
        compiler_params=_cparams(("parallel",)),
        name="dn_decode",
    )(braw, bz, small, state_conv, state_delta, cw, par, gn)


def _block_summaries(rows, kn0):
    n = rows.shape[0]
    m = jnp.mean(rows.reshape(n // CMP_BLOCK, CMP_BLOCK, LANES), axis=1)
    lane = lax.broadcasted_iota(jnp.int32, m.shape, 1)
    is_k = lane < DH_C
    ms = jnp.sum(jnp.where(is_k, m * m, 0.0), axis=-1, keepdims=True) * (1.0 / DH_C)
    return jnp.where(is_k, m * lax.rsqrt(ms + EPS) * kn0, m)


def _nsa_cmp_kernel(rows_ref, kn0_ref, o_ref):
    o_ref[...] = _block_summaries(rows_ref[...], kn0_ref[...]).astype(BF16)


def _nsa_cmp(nsa_rows, kn0, batch, seq):
    nb = seq // CMP_BLOCK
    return pl.pallas_call(
        _nsa_cmp_kernel,
        grid=(batch,),
        in_specs=[pl.BlockSpec((seq, LANES), lambda b: (b, 0)), pl.BlockSpec((1, LANES), lambda b: (0, 0))],
        out_specs=pl.BlockSpec((nb, LANES), lambda b: (b, 0)),
        out_shape=jax.ShapeDtypeStruct((batch * nb, LANES), BF16),
        compiler_params=_cparams(("parallel",)),
        name="nsa_cmp",
    )(nsa_rows, kn0)


def _alibi_slopes2(shape, axis, per_head):
    h = lax.broadcasted_iota(jnp.int32, shape, axis) // per_head
    return jnp.exp2((h + 1).astype(F32) * (-8.0 / H_C)) * LOG2E


def _masked_softmax2(s2, mask, axis):
    s2 = jnp.where(mask, s2, NEG_INF)
    m = jnp.max(s2, axis=axis, keepdims=True)
    e = jnp.where(mask, jnp.exp2(s2 - m), 0.0)
    return e / jnp.maximum(jnp.sum(e, axis=axis, keepdims=True), 1e-30)


def _stack_heads(x, width):
    return jnp.concatenate([x[:, h * width:(h + 1) * width] for h in range(H_C)], axis=0)


def _nsa_prompt_kernel(q_ref, small_ref, cmp_ref, kv_ref, gn_ref, o_ref, s_scr, *, n_blocks):
    i = pl.program_id(1)
    nr = H_C * TQ
    tk = NSA_TK
    qs = _stack_heads(q_ref[...], DH_C)
    qs_pad = jnp.concatenate([qs, jnp.zeros_like(qs)], axis=1)
    slope_r = _alibi_slopes2((nr, 1), 0, TQ)
    t_one = i * TQ + lax.broadcasted_iota(jnp.int32, (TQ, 1), 0)

    kcb = cmp_ref[:, 0:DH_C]
    vcb = cmp_ref[:, DH_C:2 * DH_C]
    t_col = i * TQ + lax.broadcasted_iota(jnp.int32, (1, nr), 1) % TQ
    slope_c = _alibi_slopes2((1, nr), 1, TQ)
    blk = lax.broadcasted_iota(jnp.int32, (n_blocks, 1), 0)
    blk_mid = (blk * CMP_BLOCK).astype(F32) + 0.5 * (CMP_BLOCK - 1)
    vis_c = ((blk + 1) * CMP_BLOCK - 1) <= t_col
    s_c = _mm(kcb, qs, _NT) - slope_c * (t_col.astype(F32) - blk_mid)
    p_c = _masked_softmax2(s_c, vis_c, 0)
    o_c = _mm(p_c.astype(BF16), vcb, _TN)

    imp = p_c[:, 0:TQ]
    for h in range(1, H_C):
        imp = imp + p_c[:, h * TQ:(h + 1) * TQ]
    cur = t_col[:, 0:TQ] // SEL_BLOCK
    jb = lax.broadcasted_iota(jnp.int32, (n_blocks, TQ), 0)
    forced = (jb == 0) | (jb == cur) | (jb == cur - 1)
    score = jnp.where(jb > cur, -1.0, jnp.where(forced, 2.0 * H_C, imp))
    rank = jnp.zeros((n_blocks, TQ), F32)
    for n in range(n_blocks):
        other = score[n:n + 1, :]
        rank = rank + ((other > score) | ((other == score) & (jb > n))).astype(F32)
    sel_t = ((rank < float(min(TOP_K_BLOCKS, n_blocks))) & (jb <= cur)).astype(BF16)
    er = lax.broadcasted_iota(jnp.int32, (n_blocks, n_blocks), 0)
    ec = lax.broadcasted_iota(jnp.int32, (n_blocks, n_blocks), 1)
    sel = _mm(sel_t, (er == ec).astype(BF16), _TN).astype(BF16)

    bpt = tk // SEL_BLOCK
    er = lax.broadcasted_iota(jnp.int32, (n_blocks, tk), 0)
    ec = lax.broadcasted_iota(jnp.int32, (n_blocks, tk), 1) // SEL_BLOCK
    s_off = lax.broadcasted_iota(jnp.int32, (1, tk), 1)
    lane = lax.broadcasted_iota(jnp.int32, (tk, LANES), 1)

    def attend(j_lo, j_hi, kv_lo, selected):
        def scores(j, mx):
            start = pl.multiple_of(j * tk, tk)
            kv = kv_ref[pl.ds(start, tk), kv_lo:kv_lo + LANES]
            s_pos = start + s_off
            valid = s_pos <= t_one
            if selected:
                valid = valid & (_mm(sel, (er == ec + j * bpt).astype(BF16)) > 0.5)
            else:
                valid = valid & (t_one - s_pos < WINDOW)
            dist = jnp.where(valid, (s_pos - t_one).astype(F32), NEG_INF)
            s2 = _mm(qs_pad, kv, _NT) + slope_r * jnp.concatenate([dist] * H_C, axis=0)
            s_scr[j] = s2
            return jnp.maximum(mx, s2)

        mx = lax.fori_loop(j_lo, j_hi, scores, jnp.full((nr, tk), NEG_INF, F32))
        m = jnp.max(mx, axis=-1, keepdims=True)

        def weigh(j, acc):
            start = pl.multiple_of(j * tk, tk)
            kv = kv_ref[pl.ds(start, tk), kv_lo:kv_lo + LANES]
            ones_v = jnp.where(lane < DH_C, 1.0, kv)
            p = jnp.exp2(s_scr[j] - m)
            return acc + _mm(p.astype(BF16), ones_v)

        acc = lax.fori_loop(j_lo, j_hi, weigh, jnp.zeros((nr, LANES), F32))
        return acc[:, DH_C:2 * DH_C] / acc[:, 0:DH_C]

    j_end = (i * TQ) // tk + 1
    o_s = attend(0, j_end, 0, True)
    j_win = jnp.maximum(i * TQ - (WINDOW - 1), 0) // tk
    o_w = attend(j_win, j_end, 2 * DH_C, False)

    gates = _sigmoid(small_ref[...])
    def gate_col(rr):
        return jnp.concatenate([gates[:, SMALL_GATE + 3 * h + rr:SMALL_GATE + 3 * h + rr + 1]
                                for h in range(H_C)], axis=0)
    o = gate_col(0) * o_c + gate_col(1) * o_s + gate_col(2) * o_w
    o = o * lax.rsqrt(jnp.mean(o * o, axis=-1, keepdims=True) + EPS)
    o = jnp.concatenate([o[h * TQ:(h + 1) * TQ] for h in range(H_C)], axis=1)
    o_ref[...] = (o * gn_ref[...]).astype(BF16)


def _nsa_prompt(cq, small, cmp, ckv, gn, batch, seq):
    nq = seq // TQ
    nb = seq // CMP_BLOCK
    row = lambda b, i: (b * nq + i, 0)
    kern = functools.partial(_nsa_prompt_kernel, n_blocks=nb)
    return pl.pallas_call(
        kern,
        grid=(batch, nq),
        in_specs=[pl.BlockSpec((TQ, W_C), row), pl.BlockSpec((TQ, LANES), row),
                  pl.BlockSpec((nb, LANES), lambda b, i: (b, 0)),
                  pl.BlockSpec((seq, 4 * DH_C), lambda b, i: (b, 0)),
                  pl.BlockSpec((1, W_C), lambda b, i: (0, 0))],
        out_specs=pl.BlockSpec((TQ, W_C), row),
        out_shape=jax.ShapeDtypeStruct((batch * seq, W_C), BF16),
        scratch_shapes=[pltpu.VMEM((seq // NSA_TK, H_C * TQ, NSA_TK), F32)],
        compiler_params=_cparams(("parallel", "arbitrary")),
        name="nsa_prompt",
    )(cq, small, cmp, ckv, gn)


def _nsa_decode_kernel(pt_ref, q_ref, small_ref, cur_nsa_ref, cur_win_ref, win_ref, kn0_ref, gn_ref, cache_ref,
                       o_ref, buf_ref, sem_ref, *, layer, n_pages, page):
    s = pl.program_id(0)
    ns = pl.num_programs(0)
    slot = s % 2

    @pl.when(s == 0)
    def _():
        for cp in _page_copies(pt_ref, cache_ref, buf_ref, sem_ref, layer, 0, 0, n_pages, page):
            cp.start()

    @pl.when(s + 1 < ns)
    def _():
        for cp in _page_copies(pt_ref, cache_ref, buf_ref, sem_ref, layer, s + 1, 1 - slot, n_pages, page):
            cp.start()

    for cp in _page_copies(pt_ref, cache_ref, buf_ref, sem_ref, layer, s, slot, n_pages, page):
        cp.wait()

    n_past = n_pages * page
    n_cmp = n_past // CMP_BLOCK
    cur = n_past // SEL_BLOCK
    t_f = float(n_past)
    hr = SUBLANES

    qrow = q_ref[pl.ds(s, 1), :]
    q8 = jnp.concatenate([qrow[:, h * DH_C:(h + 1) * DH_C] for h in range(H_C)]
                         + [jnp.zeros((hr - H_C, DH_C), F32)], axis=0).astype(BF16)
    slope = _alibi_slopes2((hr, 1), 0, 1)

    er = lax.broadcasted_iota(jnp.int32, (n_past, n_cmp), 0) // CMP_BLOCK
    ec = lax.broadcasted_iota(jnp.int32, (n_past, n_cmp), 1)
    summ = _mm_x01(buf_ref[slot, 0:2 * DH_C, :], (er == ec).astype(BF16)) * (1.0 / CMP_BLOCK)
    km = summ[0:DH_C]
    ms = jnp.sum(km * km, axis=0, keepdims=True) * (1.0 / DH_C)
    kcb = (km * lax.rsqrt(ms + EPS) * kn0_ref[...]).astype(BF16)
    vcb = summ[DH_C:2 * DH_C].astype(BF16)
    blk = lax.broadcasted_iota(jnp.int32, (1, n_cmp), 1)
    blk_mid = (blk * CMP_BLOCK).astype(F32) + 0.5 * (CMP_BLOCK - 1)
    s_c = _mm(q8, kcb) - slope * (t_f - blk_mid)
    p_c = _masked_softmax2(s_c, blk >= 0, -1)
    o_c = _mm(p_c.astype(BF16), vcb, _NT)

    hrow = lax.broadcasted_iota(jnp.int32, (hr, n_cmp), 0)
    imp = jnp.sum(jnp.where(hrow < H_C, p_c, 0.0), axis=0, keepdims=True)
    imp = jnp.concatenate([imp, jnp.zeros((1, LANES - n_cmp), F32)], axis=1)
    jb = lax.broadcasted_iota(jnp.int32, (1, LANES), 1)
    forced = (jb == 0) | (jb == cur) | (jb == cur - 1)
    score = jnp.where(jb > cur, -1.0, jnp.where(forced, 2.0 * H_C, imp))
    srow = jnp.broadcast_to(score, (LANES, LANES))
    scol = srow.T
    ri = lax.broadcasted_iota(jnp.int32, (LANES, LANES), 0)
    ci = lax.broadcasted_iota(jnp.int32, (LANES, LANES), 1)
    ahead = (scol > srow) | ((scol == srow) & (ri < ci))
    rank = jnp.sum(ahead.astype(F32), axis=0, keepdims=True)
    sel = ((rank < float(min(TOP_K_BLOCKS, cur + 1))) & (jb <= cur)).astype(BF16)
    sel8 = jnp.broadcast_to(sel, (hr, LANES))
    er = lax.broadcasted_iota(jnp.int32, (LANES, n_past), 0)
    ec = lax.broadcasted_iota(jnp.int32, (LANES, n_past), 1) // SEL_BLOCK
    picked = _mm(sel8, (er == ec).astype(BF16)) > 0.5

    ks = buf_ref[slot, 2 * DH_C:3 * DH_C, :].astype(BF16)
    vs = buf_ref[slot, 3 * DH_C:4 * DH_C, :].astype(BF16)
    pos = lax.broadcasted_iota(jnp.int32, (1, n_past), 1).astype(F32)
    s_s = _mm(q8, ks) - slope * (t_f - pos)
    cur_row = cur_nsa_ref[pl.ds(s, 1), :]
    q8f = q8.astype(F32)
    s_cur = jnp.sum(q8f * cur_row[:, 2 * DH_C:3 * DH_C].astype(BF16).astype(F32), axis=-1, keepdims=True)
    s_s = jnp.where(picked, s_s, NEG_INF)
    m = jnp.maximum(jnp.max(s_s, axis=-1, keepdims=True), s_cur)
    e = jnp.where(picked, jnp.exp2(s_s - m), 0.0)
    e_cur = jnp.exp2(s_cur - m)
    den = jnp.sum(e, axis=-1, keepdims=True) + e_cur
    o_s = (_mm(e.astype(BF16), vs, _NT) + e_cur * cur_row[:, 3 * DH_C:4 * DH_C]) / den

    n_buf = win_ref.shape[1]
    kw = win_ref[0:DH_C, :].astype(BF16)
    vw = win_ref[DH_C:2 * DH_C, :].astype(BF16)
    wpos = n_past - n_buf + lax.broadcasted_iota(jnp.int32, (1, n_buf), 1)
    vis_w = (n_past - wpos < WINDOW) & (wpos >= 0)
    s_w = _mm(q8, kw) - slope * (t_f - wpos.astype(F32))
    cur_w = cur_win_ref[pl.ds(s, 1), :]
    sw_cur = jnp.sum(q8f * cur_w[:, 0:DH_C].astype(BF16).astype(F32), axis=-1, keepdims=True)
    s_w = jnp.where(vis_w, s_w, NEG_INF)
    m = jnp.maximum(jnp.max(s_w, axis=-1, keepdims=True), sw_cur)
    e = jnp.where(vis_w, jnp.exp2(s_w - m), 0.0)
    e_cur = jnp.exp2(sw_cur - m)
    den = jnp.sum(e, axis=-1, keepdims=True) + e_cur
    o_w = (_mm(e.astype(BF16), vw, _NT) + e_cur * cur_w[:, DH_C:2 * DH_C]) / den

    gates = jnp.broadcast_to(_sigmoid(small_ref[pl.ds(s, 1), :]), (hr, LANES))
    gl = lax.broadcasted_iota(jnp.int32, (hr, LANES), 1)
    gh = lax.broadcasted_iota(jnp.int32, (hr, LANES), 0)

    def gate_col(rr):
        return jnp.sum(jnp.where(gl == SMALL_GATE + 3 * gh + rr, gates, 0.0), axis=-1, keepdims=True)

    o = gate_col(0) * o_c + gate_col(1) * o_s + gate_col(2) * o_w
    o = o * lax.rsqrt(jnp.mean(o * o, axis=-1, keepdims=True) + EPS)
    orow = jnp.concatenate([o[h:h + 1] for h in range(H_C)], axis=1)
    o_ref[pl.ds(s, 1), :] = orow * gn_ref[...]


def _nsa_decode(page_table, cq, small, cur_nsa, cur_win, state_win, kn0, gn, cache, layer):
    n_seq, n_pages = page_table.shape
    page = cache.shape[3]
    n_buf = state_win.shape[3]
    kern = functools.partial(_nsa_decode_kernel, layer=layer, n_pages=n_pages, page=page)
    whole = lambda shape: pl.BlockSpec(shape, lambda s, pt: (0,) * len(shape))
    return pl.pallas_call(
        kern,
        grid_spec=pltpu.PrefetchScalarGridSpec(
            num_scalar_prefetch=1,
            grid=(n_seq,),
            in_specs=[whole((n_seq, W_C)), whole((n_seq, LANES)), whole((n_seq, 4 * DH_C)), whole((n_seq, 2 * DH_C)),
                      pl.BlockSpec((None, None, 2 * DH_C, n_buf), lambda s, pt: (s, layer, 0, 0)),
                      whole((DH_C, 1)), whole((1, W_C)),
                      pl.BlockSpec(memory_space=pl.ANY)],
            out_specs=whole((n_seq, W_C)),
            scratch_shapes=[pltpu.VMEM((2, 4 * DH_C, n_pages * page), F32),
                            pltpu.SemaphoreType.DMA((2,))]),
        out_shape=jax.ShapeDtypeStruct((n_seq, W_C), F32),
        compiler_params=_cparams(("arbitrary",)),
        name="nsa_decode",
    )(page_table, cq, small, cur_nsa, cur_win, state_win, kn0, gn, cache)


def _pack_w_in(w_in, layer):
    wt = jnp.transpose(w_in, (2, 0, 1))[:, layer, :]
    widths = [W_A, W_A, W_A, H_B * DK_B, H_B * DK_B, W_B, W_B, H_B, H_B, W_C] + [DH_C] * 6 + [3 * H_C]
    rows, start = [], 0
    for w in widths:
        rows.append(wt[start:start + w])
        start += w
    (a_q, a_k, a_v, b_q, b_k, b_v, b_z, b_beta, b_decay, c_q, c_kc, c_vc, c_ks, c_vs, c_kw, c_vw, c_gate) = rows
    pad = jnp.zeros((LANES - 2 * H_B - 3 * H_C, wt.shape[1]), wt.dtype)
    packed = jnp.concatenate([a_k, a_v, a_q, b_q, b_k, b_v, b_z, c_q, c_kc, c_vc, c_ks, c_vs, c_kw, c_vw,
                              b_beta, b_decay, c_gate, pad], axis=0)
    return packed.astype(BF16)


def _row_tile(t, cap):
    tm = min(t, cap)
    while t % tm:
        tm //= 2
    return tm


def kernel(x_prompt, x_sample, cache_sb, cache_nsa, state_win, state_delta, state_conv, page_table, w_in, conv_w,
           a_log, dt_bias, b_out_norm, a_out_norm, c_q_norm, c_k_norm, c_out_norm, w_out, attn_norm, ffn_norm,
           dense_w_gu, dense_w_down, router_w, router_b, moe_w_gu, moe_w_down):
    batch, seq, _ = x_prompt.shape
    n_dec = x_sample.shape[0]
    depth = w_in.shape[0]
    n_phys, _, page = cache_sb.shape[:3]
    n_buf = state_win.shape[2]
    tp, ts = batch * seq, n_dec
    tm_p, tm_s = _row_tile(tp, 512), _row_tile(ts, 512)

    cache_sb2 = jnp.transpose(cache_sb, (0, 1, 3, 4, 5, 2)).reshape(n_phys, depth, 2 * W_A, page)
    cache_nsa2 = jnp.transpose(cache_nsa, (0, 1, 3, 4, 2)).reshape(n_phys, depth, 4 * DH_C, page)
    state_win2 = jnp.transpose(state_win, (0, 1, 3, 4, 2)).reshape(n_dec, depth, 2 * DH_C, n_buf)
    d_ff = dense_w_down.shape[1]
    tf = d_ff // 2 if (d_ff // 2) % LANES == 0 else d_ff

    xp = x_prompt.reshape(tp, D_MODEL)
    xs = x_sample.reshape(ts, D_MODEL)
    outs_p = {k: [] for k in ("sb", "nsa", "win", "delta", "conv")}
    outs_s = {k: [] for k in ("sb", "nsa", "win", "delta", "conv")}
    for l in range(depth):
        w_pk = _pack_w_in(w_in, l)
        wo = w_out[l].astype(BF16)
        g_attn = attn_norm[l].reshape(1, D_MODEL)
        g_ffn = ffn_norm[l].reshape(1, D_MODEL)
        qn = jnp.tile(c_q_norm[l], H_C).reshape(1, W_C)
        ones64 = jnp.ones((DH_C,), F32)
        kn = jnp.concatenate([ones64, ones64, c_k_norm[l, 1], ones64, c_k_norm[l, 2], ones64]).reshape(1, 384)
        kn0 = jnp.concatenate([c_k_norm[l, 0], ones64]).reshape(1, LANES)
        par = (jnp.zeros((2, LANES), F32).at[0, SMALL_DECAY:SMALL_DECAY + H_B].set(a_log[l])
               .at[1, SMALL_DECAY:SMALL_DECAY + H_B].set(dt_bias[l]))
        gn_a = a_out_norm[l].reshape(1, W_A)
        gn_b = b_out_norm[l].reshape(1, DV_B)
        gn_c = c_out_norm[l].reshape(1, W_C)
        cw = conv_w[l]

        sb, aq, akv, braw, bz, small, cq, nsa, win, ckv = _inproj(xp, g_attn, w_pk, qn, kn, tm_p)
        oa = _sb_prompt(aq, akv, gn_a, batch, seq)
        ob, s_fin = _dn_prompt(braw, bz, small, cw, par, gn_b, batch, seq)
        cmp = _nsa_cmp(nsa, kn0, batch, seq)
        oc = _nsa_prompt(cq, small, cmp, ckv, gn_c, batch, seq)
        outs_p["sb"].append(sb.reshape(batch, seq, 2, H_A, DH_A))
        outs_p["nsa"].append(nsa.reshape(batch, seq, 4, DH_C))
        outs_p["win"].append(win.reshape(batch, seq, 2, DH_C)[:, seq - min(WINDOW, seq):])
        outs_p["delta"].append(s_fin)
        outs_p["conv"].append(braw.reshape(batch, seq, CONV_CH)[:, seq - (CONV_W - 1):])

        sb_s, aq_s, _, braw_s, bz_s, small_s, cq_s, nsa_s, win_s, _ = _inproj(xs, g_attn, w_pk, qn, kn, tm_s)
        oa_s = _sb_decode(page_table, aq_s.astype(F32), gn_a, cache_sb2, l).astype(BF16)
        ob_s, s_new = _dn_decode(braw_s, bz_s, small_s, state_conv, state_delta, cw, par, gn_b, l)
        oc_s = _nsa_decode(page_table, cq_s.astype(F32), small_s, nsa_s, win_s, state_win2,
                           c_k_norm[l, 0].reshape(DH_C, 1), gn_c, cache_nsa2, l).astype(BF16)
        outs_s["sb"].append(sb_s.reshape(n_dec, 1, 2, H_A, DH_A))
        outs_s["nsa"].append(nsa_s.reshape(n_dec, 1, 4, DH_C))
        win_all = jnp.concatenate([state_win[:, l], win_s.reshape(n_dec, 1, 2, DH_C)], axis=1)
        outs_s["win"].append(win_all[:, win_all.shape[1] - n_buf:])
        outs_s["delta"].append(s_new)
        outs_s["conv"].append(jnp.concatenate([state_conv[:, l], braw_s.reshape(n_dec, 1, CONV_CH)], axis=1)[:, 1:])

        if l % 2 == 0:
            wgu = dense_w_gu[l // 2].astype(BF16)
            wd = dense_w_down[l // 2].astype(BF16)
            xp = _ffn_dense(xp, oa, ob, oc, wo, g_ffn, wgu, wd, tm_p, tf)
            xs = _ffn_dense(xs, oa_s, ob_s, oc_s, wo, g_ffn, wgu, wd, tm_s, tf)
        else:
            wr = jnp.pad(router_w[l // 2], ((0, 0), (0, LANES - N_EXPERTS)))
            br = jnp.pad(router_b[l // 2], (0, LANES - N_EXPERTS)).reshape(1, LANES)
            wgu = moe_w_gu[l // 2].astype(BF16)
            wd = moe_w_down[l // 2].astype(BF16)
            xp = _ffn_moe(xp, oa, ob, oc, wo, g_ffn, wr, br, wgu, wd, tm_p)
            xs = _ffn_moe(xs, oa_s, ob_s, oc_s, wo, g_ffn, wr, br, wgu, wd, tm_s)

    stack = lambda xs_: jnp.stack(xs_, axis=1)
    return (xp.reshape(batch, seq, D_MODEL), xs.reshape(n_dec, 1, D_MODEL),
            stack(outs_p["sb"]), stack(outs_s["sb"]), stack(outs_p["nsa"]), stack(outs_s["nsa"]),
            stack(outs_p["win"]), stack(outs_s["win"]), stack(outs_p["delta"]), stack(outs_s["delta"]),
            stack(outs_p["conv"]), stack(outs_s["conv"]))
```

```python
import functools

import jax
import jax.numpy as jnp
from jax import lax
from jax.experimental import pallas as pl
from jax.experimental.pallas import tpu as pltpu

F32 = jnp.float32
BF16 = jnp.bfloat16

D_MODEL = 1024
H_A, DH_A = 4, 64
H_B, DK_B, DV_B = 4, 128, 128
H_C, DH_C = 4, 64
W_A = H_A * DH_A
W_B = H_B * DV_B
W_C = H_C * DH_C
CONV_W = 4
CONV_CH = 2 * H_B * DK_B + W_B
DN_CHUNK = 64
CMP_BLOCK = 64
SEL_BLOCK = 64
TOP_K_BLOCKS = 16
WINDOW = 512
N_EXPERTS = 8
EPS = 1e-6
NEG_INF = -1e30

LANES = 128
SUBLANES = 8
VMEM_LIMIT = 56 * 1024 * 1024

N_PROJ = 512 + 256 + CONV_CH + W_B + W_C + 256 + 128 + 128
SMALL_BETA, SMALL_DECAY, SMALL_GATE = 0, H_B, 2 * H_B
TQ = 128
SB_TK = 256
NSA_TK = 256
LOG2E = 1.4426950408889634
DN_BLOCK = 256


def _cparams(sem):
    return pltpu.CompilerParams(dimension_semantics=sem, vmem_limit_bytes=VMEM_LIMIT)


def _split2(x):
    hi = x.astype(BF16)
    lo = (x - hi.astype(F32)).astype(BF16)
    return hi, lo


def _split3(x):
    hi = x.astype(BF16)
    r = x - hi.astype(F32)
    mid = r.astype(BF16)
    lo = (r - mid.astype(F32)).astype(BF16)
    return hi, mid, lo


_NN = (((1,), (0,)), ((), ()))
_NT = (((1,), (1,)), ((), ()))
_TN = (((0,), (0,)), ((), ()))


def _mm(a, b, dims=_NN):
    return lax.dot_general(a, b, dims, preferred_element_type=F32)


def _mm_x01(x, m01, parts=2):
    ps = _split2(x) if parts == 2 else _split3(x)
    out = _mm(ps[0], m01)
    for p in ps[1:]:
        out = out + _mm(p, m01)
    return out


def _mm_01x(m01, x, parts=2):
    ps = _split2(x) if parts == 2 else _split3(x)
    out = _mm(m01, ps[0])
    for p in ps[1:]:
        out = out + _mm(m01, p)
    return out


def _mm3(a, b, dims=_NN):
    ah, al = _split2(a)
    bh, bl = _split2(b)
    return _mm(ah, bh, dims) + (_mm(ah, bl, dims) + _mm(al, bh, dims))


def _silu(x):
    return x * (1.0 / (1.0 + jnp.exp(-x)))


def _sigmoid(x):
    return 1.0 / (1.0 + jnp.exp(-x))


def _softplus(x):
    return jnp.maximum(x, 0.0) + jnp.log(1.0 + jnp.exp(-jnp.abs(x)))


def _group_ones(n, group):
    r = lax.broadcasted_iota(jnp.int32, (n, n), 0) // group
    c = lax.broadcasted_iota(jnp.int32, (n, n), 1) // group
    return (r == c).astype(BF16)


def _inproj_kernel(x_ref, g_ref, w_ref, qn_ref, kn_ref,
                   sb_ref, aq_ref, akv_ref, braw_ref, bz_ref, small_ref, cq_ref, nsa_ref, win_ref, ckv_ref,
                   cqt_ref, ckvt_ref):
    x = x_ref[...]
    h = x * lax.rsqrt(jnp.mean(x * x, axis=-1, keepdims=True) + EPS) * g_ref[...]
    p = _mm(h.astype(BF16), w_ref[...], _NT)
    o = 0
    sb = p[:, o:o + 512]; o += 512
    aq = p[:, o:o + 256]; o += 256
    braw = p[:, o:o + CONV_CH]; o += CONV_CH
    bz = p[:, o:o + W_B]; o += W_B
    cq = p[:, o:o + W_C]; o += W_C
    nsa = p[:, o:o + 256]; o += 256
    win = p[:, o:o + 128]; o += 128
    small = p[:, o:o + 128]
    sb_ref[...] = sb
    aq_ref[...] = (aq * (DH_A ** -0.5 * LOG2E)).astype(BF16)
    akv_ref[...] = sb.astype(BF16)
    braw_ref[...] = braw
    bz_ref[...] = bz
    small_ref[...] = small
    g64 = _group_ones(256, DH_C)
    ms = _mm_x01(cq * cq, g64) * (1.0 / DH_C)
    cqn = cq * lax.rsqrt(ms + EPS) * qn_ref[...]
    cq_ref[...] = (cqn * (DH_C ** -0.5 * LOG2E)).astype(BF16)
    lane = lax.broadcasted_iota(jnp.int32, nsa.shape, 1)
    is_ks = (lane >= 128) & (lane < 192)
    ms = _mm_x01(nsa * nsa, g64) * (1.0 / DH_C)
    nsa_n = jnp.where(is_ks, nsa * lax.rsqrt(ms + EPS) * kn_ref[:, 0:256], nsa)
    nsa_ref[...] = nsa_n
    lane = lax.broadcasted_iota(jnp.int32, win.shape, 1)
    ms = _mm_x01(win * win, g64[0:128, 0:128]) * (1.0 / DH_C)
    win_n = jnp.where(lane < 64, win * lax.rsqrt(ms + EPS) * kn_ref[:, 256:384], win)
    win_ref[...] = win_n
    ckv = jnp.concatenate([nsa_n[:, 128:256], win_n], axis=1)
    ckv_ref[...] = ckv.astype(BF16)
    cqt_ref[...] = (cqn * (DH_C ** -0.5 * LOG2E)).T.astype(BF16)
    ckvt_ref[...] = ckv.T.astype(BF16)


def _inproj(x, g, w, qn, kn, tm):
    t = x.shape[0]
    row = lambda i: (i, 0)
    const = lambda i: (0, 0)
    outs = [(512, F32), (256, BF16), (512, BF16), (CONV_CH, F32), (W_B, F32), (128, F32),
            (W_C, BF16), (256, F32), (128, F32), (256, BF16)]
    return pl.pallas_call(
        _inproj_kernel,
        grid=(t // tm,),
        in_specs=[pl.BlockSpec((tm, D_MODEL), row), pl.BlockSpec((1, D_MODEL), const),
                  pl.BlockSpec((N_PROJ, D_MODEL), const), pl.BlockSpec((1, 256), const),
                  pl.BlockSpec((1, 384), const)],
        out_specs=[pl.BlockSpec((tm, n), row) for n, _ in outs]
        + [pl.BlockSpec((W_C, tm), lambda i: (0, i)), pl.BlockSpec((4 * DH_C, tm), lambda i: (0, i))],
        out_shape=[jax.ShapeDtypeStruct((t, n), d) for n, d in outs]
        + [jax.ShapeDtypeStruct((W_C, t), BF16), jax.ShapeDtypeStruct((4 * DH_C, t), BF16)],
        compiler_params=_cparams(("parallel",)),
        name="inproj",
    )(x, g, w, qn, kn)


def _mix_prologue(x_ref, oa_ref, ob_ref, oc_ref, wo_ref, g_ref, xn_scr, h_scr):
    mix = (_mm(oa_ref[...], wo_ref[0:W_A, :]) + _mm(ob_ref[...], wo_ref[W_A:W_A + W_B, :])
           + _mm(oc_ref[...], wo_ref[W_A + W_B:, :]))
    xn = x_ref[...] + mix
    xn_scr[...] = xn
    h = xn * lax.rsqrt(jnp.mean(xn * xn, axis=-1, keepdims=True) + EPS) * g_ref[...]
    h_scr[...] = h.astype(BF16)
    return h


def _ffn_dense_kernel(x_ref, oa_ref, ob_ref, oc_ref, wo_ref, g_ref, wg_ref, wu_ref, wd_ref, o_ref,
                      xn_scr, h_scr, acc_scr):
    f = pl.program_id(1)

    @pl.when(f == 0)
    def _():
        _mix_prologue(x_ref, oa_ref, ob_ref, oc_ref, wo_ref, g_ref, xn_scr, h_scr)
        acc_scr[...] = jnp.zeros_like(acc_scr)

    h = h_scr[...]
    act = _silu(_mm(h, wg_ref[...])) * _mm(h, wu_ref[...])
    acc_scr[...] += _mm(act.astype(BF16), wd_ref[...])

    @pl.when(f == pl.num_programs(1) - 1)
    def _():
        o_ref[...] = xn_scr[...] + acc_scr[...]


def _ffn_dense(x, oa, ob, oc, wo, g, wgu, wd, tm, tf):
    t = x.shape[0]
    d_ff = wd.shape[0]
    nf = d_ff // tf
    row = lambda i, f: (i, 0)
    const = lambda i, f: (0, 0)
    return pl.pallas_call(
        _ffn_dense_kernel,
        grid=(t // tm, nf),
        in_specs=[pl.BlockSpec((tm, D_MODEL), row), pl.BlockSpec((tm, W_A), row), pl.BlockSpec((tm, W_B), row),
                  pl.BlockSpec((tm, W_C), row), pl.BlockSpec((D_MODEL, D_MODEL), const),
                  pl.BlockSpec((1, D_MODEL), const),
                  pl.BlockSpec((D_MODEL, tf), lambda i, f: (0, f)),
                  pl.BlockSpec((D_MODEL, tf), lambda i, f: (0, f + nf)),
                  pl.BlockSpec((tf, D_MODEL), lambda i, f: (f, 0))],
        out_specs=pl.BlockSpec((tm, D_MODEL), row),
        out_shape=jax.ShapeDtypeStruct((t, D_MODEL), F32),
        scratch_shapes=[pltpu.VMEM((tm, D_MODEL), F32), pltpu.VMEM((tm, D_MODEL), BF16),
                        pltpu.VMEM((tm, D_MODEL), F32)],
        compiler_params=_cparams(("parallel", "arbitrary")),
        name="ffn_dense",
    )(x, oa, ob, oc, wo, g, wgu, wgu, wd)


def _ffn_moe_kernel(x_ref, oa_ref, ob_ref, oc_ref, wo_ref, g_ref, wr_ref, br_ref, wg_ref, wu_ref, wd_ref, o_ref,
                    xn_scr, h_scr, acc_scr, gate_scr):
    e = pl.program_id(1)

    @pl.when(e == 0)
    def _():
        h = _mix_prologue(x_ref, oa_ref, ob_ref, oc_ref, wo_ref, g_ref, xn_scr, h_scr)
        acc_scr[...] = jnp.zeros_like(acc_scr)
        logits = _mm3(h, wr_ref[...]) + br_ref[...]
        lane = lax.broadcasted_iota(jnp.int32, logits.shape, 1)
        logits = jnp.where(lane < N_EXPERTS, logits, -jnp.inf)
        m1 = jnp.max(logits, axis=-1, keepdims=True)
        i1 = jnp.min(jnp.where(logits == m1, lane, LANES), axis=-1, keepdims=True)
        rest = jnp.where(lane == i1, -jnp.inf, logits)
        m2 = jnp.max(rest, axis=-1, keepdims=True)
        i2 = jnp.min(jnp.where(rest == m2, lane, LANES), axis=-1, keepdims=True)
        e2 = jnp.exp(m2 - m1)
        den = 1.0 + e2
        gate_scr[...] = jnp.where(lane == i1, 1.0 / den, 0.0) + jnp.where(lane == i2, e2 / den, 0.0)

    h = h_scr[...]
    act = _silu(_mm(h, wg_ref[...])) * _mm(h, wu_ref[...])
    y = _mm(act.astype(BF16), wd_ref[...])
    lane = lax.broadcasted_iota(jnp.int32, gate_scr.shape, 1)
    ge = jnp.sum(jnp.where(lane == e, gate_scr[...], 0.0), axis=-1, keepdims=True)
    acc_scr[...] += ge * y

    @pl.when(e == pl.num_programs(1) - 1)
    def _():
        o_ref[...] = xn_scr[...] + acc_scr[...]


def _ffn_moe(x, oa, ob, oc, wo, g, wr, br, wgu, wd, tm):
    t = x.shape[0]
    ne, _, two_f = wgu.shape
    f = two_f // 2
    row = lambda i, e: (i, 0)
    const = lambda i, e: (0, 0)
    return pl.pallas_call(
        _ffn_moe_kernel,
        grid=(t // tm, ne),
        in_specs=[pl.BlockSpec((tm, D_MODEL), row), pl.BlockSpec((tm, W_A), row), pl.BlockSpec((tm, W_B), row),
                  pl.BlockSpec((tm, W_C), row), pl.BlockSpec((D_MODEL, D_MODEL), const),
                  pl.BlockSpec((1, D_MODEL), const), pl.BlockSpec((D_MODEL, LANES), const),
                  pl.BlockSpec((1, LANES), const),
                  pl.BlockSpec((None, D_MODEL, f), lambda i, e: (e, 0, 0)),
                  pl.BlockSpec((None, D_MODEL, f), lambda i, e: (e, 0, 1)),
                  pl.BlockSpec((None, f, D_MODEL), lambda i, e: (e, 0, 0))],
        out_specs=pl.BlockSpec((tm, D_MODEL), row),
        out_shape=jax.ShapeDtypeStruct((t, D_MODEL), F32),
        scratch_shapes=[pltpu.VMEM((tm, D_MODEL), F32), pltpu.VMEM((tm, D_MODEL), BF16),
                        pltpu.VMEM((tm, D_MODEL), F32), pltpu.VMEM((tm, LANES), F32)],
        compiler_params=_cparams(("parallel", "arbitrary")),
        name="ffn_moe",
    )(x, oa, ob, oc, wo, g, wr, br, wgu, wgu, wd)


def _sb_log2_terms(z2):
    sp = jnp.log2(1.0 + jnp.exp2(-jnp.abs(z2)))
    ls = jnp.minimum(z2, 0.0) - sp
    return ls, ls - z2


def _sb_prompt_kernel(q_ref, kv_ref, gn_ref, o_ref):
    i = pl.program_id(1)
    tk = SB_TK
    nr = H_A * TQ
    r = lax.broadcasted_iota(jnp.int32, (tk, tk), 0)
    c = lax.broadcasted_iota(jnp.int32, (tk, tk), 1)
    later01 = (r > c).astype(BF16)
    t_row = i * TQ + lax.broadcasted_iota(jnp.int32, (nr, 1), 0) % TQ
    s_off = lax.broadcasted_iota(jnp.int32, (1, tk), 1)
    qs = [q_ref[:, h * DH_A:(h + 1) * DH_A] for h in range(H_A)]

    def terms(j, diag):
        start = pl.multiple_of(j * tk, tk)
        z2 = jnp.concatenate([_mm(qs[h], kv_ref[pl.ds(start, tk), h * DH_A:(h + 1) * DH_A], _NT)
                              for h in range(H_A)], axis=0)
        ls, lk = _sb_log2_terms(z2)
        before = (start + s_off) < t_row if diag else None
        if diag:
            lk = jnp.where(before, lk, 0.0)
        return ls + _mm(lk.astype(BF16), later01), jnp.sum(lk, axis=-1, keepdims=True), before

    def weighted_values(j, logw, before):
        start = pl.multiple_of(j * tk, tk)
        w = jnp.exp2(logw)
        if before is not None:
            w = jnp.where(before, w, 0.0)
        wh = w.astype(BF16)
        return jnp.concatenate([_mm(wh[h * TQ:(h + 1) * TQ],
                                    kv_ref[pl.ds(start, tk), W_A + h * DH_A:W_A + (h + 1) * DH_A])
                                for h in range(H_A)], axis=0)

    def step(j, carry, diag):
        run, acc = carry
        lw, tot, before = terms(j, diag)
        return run + tot, acc + weighted_values(j, lw + run, before)

    def two_steps(j, carry):
        run, acc = carry
        lw_a, tot_a, _ = terms(j, False)
        lw_b, tot_b, _ = terms(j - 1, False)
        run_b = run + tot_a
        acc = acc + weighted_values(j, lw_a + run, None) + weighted_values(j - 1, lw_b + run_b, None)
        return run_b + tot_b, acc

    jd = i // (tk // TQ)
    carry = (jnp.zeros((nr, 1), F32), jnp.zeros((nr, DH_A), F32))
    carry = step(jd, carry, True)
    carry = lax.fori_loop(0, jd // 2, lambda p, cr: two_steps(jd - 1 - 2 * p, cr), carry)
    carry = lax.cond(jd % 2 == 1, lambda cr: step(0, cr, False), lambda cr: cr, carry)
    o = carry[1]
    o = o * lax.rsqrt(jnp.mean(o * o, axis=-1, keepdims=True) + EPS)
    o = jnp.concatenate([o[h * TQ:(h + 1) * TQ] for h in range(H_A)], axis=1)
    o_ref[...] = (o * gn_ref[...]).astype(BF16)


def _sb_prompt(aq, akv, gn, batch, seq):
    nq = seq // TQ
    return pl.pallas_call(
        _sb_prompt_kernel,
        grid=(batch, nq),
        in_specs=[pl.BlockSpec((TQ, W_A), lambda b, i: (b * nq + i, 0)),
                  pl.BlockSpec((seq, 2 * W_A), lambda b, i: (b, 0)),
                  pl.BlockSpec((1, W_A), lambda b, i: (0, 0))],
        out_specs=pl.BlockSpec((TQ, W_A), lambda b, i: (b * nq + i, 0)),
        out_shape=jax.ShapeDtypeStruct((batch * seq, W_A), BF16),
        compiler_params=_cparams(("parallel", "arbitrary")),
        name="sb_prompt",
    )(aq, akv, gn)


SB_DEC_GROUP = 2
NSA_DEC_GROUP = 4


def _page_copies(pt_ref, cache_ref, buf_ref, sem_ref, layer, step, slot, n_pages, page):
    group = buf_ref.shape[1]
    return [pltpu.make_async_copy(cache_ref.at[pt_ref[step * group + g, p], layer],
                                  buf_ref.at[slot, g, :, pl.ds(p * page, page)], sem_ref.at[slot])
            for g in range(group) for p in range(n_pages)]


def _interleave(stage_generators):
    live = list(stage_generators)
    while live:
        live = [gen for gen in live if next(gen, _DONE) is not _DONE]


_DONE = object()


def _paged_prefetch(pt_ref, cache_ref, buf_ref, sem_ref, layer, n_pages, page):
    s = pl.program_id(0)
    slot = s % 2

    @pl.when(s == 0)
    def _():
        for cp in _page_copies(pt_ref, cache_ref, buf_ref, sem_ref, layer, 0, 0, n_pages, page):
            cp.start()

    @pl.when(s + 1 < pl.num_programs(0))
    def _():
        for cp in _page_copies(pt_ref, cache_ref, buf_ref, sem_ref, layer, s + 1, 1 - slot, n_pages, page):
            cp.start()

    for cp in _page_copies(pt_ref, cache_ref, buf_ref, sem_ref, layer, s, slot, n_pages, page):
        cp.wait()
    return slot


def _chunk_rows(x, n_chunks):
    return jnp.concatenate([x[:, c * LANES:(c + 1) * LANES] for c in range(n_chunks)], axis=0)


def _chunk_lanes(x, n_chunks):
    return jnp.concatenate([x[c * SUBLANES:(c + 1) * SUBLANES] for c in range(n_chunks)], axis=1)


def _sb_decode_kernel(pt_ref, q_ref, gn_ref, cache_ref, o_ref, buf_ref, sem_ref, *, layer, n_pages, page):
    slot = _paged_prefetch(pt_ref, cache_ref, buf_ref, sem_ref, layer, n_pages, page)
    group = buf_ref.shape[1]
    hr = SUBLANES
    nc = n_pages * page // LANES
    rr = lax.broadcasted_iota(jnp.int32, (hr, W_A), 0)
    cc = lax.broadcasted_iota(jnp.int32, (hr, W_A), 1)
    own = rr == cc // DH_A
    r = lax.broadcasted_iota(jnp.int32, (LANES, LANES), 0)
    c = lax.broadcasted_iota(jnp.int32, (LANES, LANES), 1)
    later01 = (r > c).astype(BF16)
    r = lax.broadcasted_iota(jnp.int32, (hr * nc, hr * nc), 0)
    c = lax.broadcasted_iota(jnp.int32, (hr * nc, hr * nc), 1)
    later_chunk01 = ((r % hr == c % hr) & (c // hr > r // hr)).astype(BF16)
    group1 = _group_ones(W_A, DH_A)

    def one_sequence(g):
        s = pl.program_id(0) * group + g
        q = q_ref[pl.ds(s, 1), :]
        q8 = jnp.where(own, q, 0.0).astype(BF16)
        z2 = _mm(q8, buf_ref[slot, g, 0:W_A, :].astype(BF16))
        yield
        ls, lk = _sb_log2_terms(z2)
        ls_c, lk_c = _chunk_rows(ls, nc), _chunk_rows(lk, nc)
        inner = _mm_x01(lk_c, later01)
        tot = jnp.broadcast_to(jnp.sum(lk_c, axis=-1, keepdims=True), (hr * nc, LANES))
        yield
        outer = _mm_01x(later_chunk01, tot)
        w = _chunk_lanes(jnp.exp2(ls_c + inner + outer), nc)
        yield
        o8 = _mm(w.astype(BF16), buf_ref[slot, g, W_A:2 * W_A, :].astype(BF16), _NT)
        yield
        o = jnp.sum(jnp.where(own, o8, 0.0), axis=0, keepdims=True)
        ms = _mm_x01(o * o, group1) * (1.0 / DH_A)
        o_ref[pl.ds(s, 1), :] = o * lax.rsqrt(ms + EPS) * gn_ref[...]

    _interleave([one_sequence(g) for g in range(group)])


def _sb_decode(page_table, aq, gn, cache, layer):
    n_seq, n_pages = page_table.shape
    page = cache.shape[3]
    kern = functools.partial(_sb_decode_kernel, layer=layer, n_pages=n_pages, page=page)
    return pl.pallas_call(
        kern,
        grid_spec=pltpu.PrefetchScalarGridSpec(
            num_scalar_prefetch=1,
            grid=(n_seq // SB_DEC_GROUP,),
            in_specs=[pl.BlockSpec((n_seq, W_A), lambda s, pt: (0, 0)),
                      pl.BlockSpec((1, W_A), lambda s, pt: (0, 0)),
                      pl.BlockSpec(memory_space=pl.ANY)],
            out_specs=pl.BlockSpec((n_seq, W_A), lambda s, pt: (0, 0)),
            scratch_shapes=[pltpu.VMEM((2, SB_DEC_GROUP, 2 * W_A, n_pages * page), F32),
                            pltpu.SemaphoreType.DMA((2,))]),
        out_shape=jax.ShapeDtypeStruct((n_seq, W_A), F32),
        compiler_params=_cparams(("arbitrary",)),
        name="sb_decode",
    )(page_table, aq, gn, cache)


def _unit_lower_inverse(a_mats, n):
    r = lax.broadcasted_iota(jnp.int32, (n, n), 0)
    c = lax.broadcasted_iota(jnp.int32, (n, n), 1)
    eye = (r == c).astype(F32)
    base = 16
    xs = [jnp.where(r // base == c // base, -a, 0.0) for a in a_mats]
    invs = [eye + x for x in xs]
    span = 1
    while 2 * span < base:
        xs = [_mm3(x, x) for x in xs]
        invs = [_mm3(inv, eye + x) for inv, x in zip(invs, xs)]
        span *= 2
    size = base
    while size < DN_CHUNK:
        off_mask = (r // (2 * size) == c // (2 * size)) & (r // size != c // size)
        mids = [_mm3(jnp.where(off_mask, a, 0.0), inv) for a, inv in zip(a_mats, invs)]
        invs = [inv - _mm3(inv, mid) for inv, mid in zip(invs, mids)]
        size *= 2
    return invs


def _l2n(x):
    return x * lax.rsqrt(jnp.sum(x * x, axis=-1, keepdims=True) + EPS)


def _dn_prompt_kernel(raw_ref, z_ref, small_ref, cw_ref, par_ref, gn_ref, o_ref, s_ref, xbuf, s_scr):
    ci = pl.program_id(1)
    tb = DN_BLOCK
    nk = H_B * DK_B

    @pl.when(ci == 0)
    def _():
        xbuf[0:SUBLANES, :] = jnp.zeros((SUBLANES, CONV_CH), F32)
        s_scr[...] = jnp.zeros_like(s_scr)

    xbuf[SUBLANES:SUBLANES + tb, :] = raw_ref[...]
    conv = xbuf[SUBLANES - 3:SUBLANES - 3 + tb, :] * cw_ref[0:1, :]
    for i in range(1, CONV_W):
        conv = conv + xbuf[SUBLANES - 3 + i:SUBLANES - 3 + i + tb, :] * cw_ref[i:i + 1, :]
    xbuf[0:SUBLANES, :] = xbuf[tb:tb + SUBLANES, :]
    qkv = _silu(conv)

    small = small_ref[...]
    beta_all = _sigmoid(small)
    g_all = -jnp.exp(par_ref[0:1, :]) * _softplus(small + par_ref[1:2, :])

    r = lax.broadcasted_iota(jnp.int32, (tb, tb), 0)
    c = lax.broadcasted_iota(jnp.int32, (tb, tb), 1)
    same = (r // DN_CHUNK) == (c // DN_CHUNK)
    lower = same & (r >= c)
    strict = same & (r > c)
    lower01 = lower.astype(BF16)
    upper01 = (same & (r <= c)).astype(BF16)
    same01 = same.astype(BF16)
    ones8 = jnp.ones((SUBLANES, tb), BF16)

    g_cum = _mm_01x(lower01, g_all, parts=3)
    g_tot = _mm_01x(same01, g_all, parts=3)
    e_cum = jnp.exp(g_cum)
    e_rest = jnp.exp(g_tot - g_cum)
    e_tot = jnp.exp(g_tot)

    heads = range(H_B)
    lanes = [SMALL_DECAY + h for h in heads]
    qs = [_l2n(qkv[:, h * DK_B:(h + 1) * DK_B]) * (DK_B ** -0.5) for h in heads]
    ks = [_l2n(qkv[:, nk + h * DK_B:nk + (h + 1) * DK_B]) for h in heads]
    vs = [qkv[:, 2 * nk + h * DV_B:2 * nk + (h + 1) * DV_B] for h in heads]
    betas = [beta_all[:, SMALL_BETA + h:SMALL_BETA + h + 1] for h in heads]
    upper_f = upper01.astype(F32)
    gc_rows = [_mm_01x(ones8, g_all[:, ln:ln + 1] * upper_f, parts=3)[0:1, :] for ln in lanes]
    decays = [jnp.exp(jnp.where(lower, g_cum[:, ln:ln + 1] - gr, -jnp.inf)) for ln, gr in zip(lanes, gc_rows)]
    kbs = [k * b for k, b in zip(ks, betas)]
    khs = [k.astype(BF16) for k in ks]
    qhs = [q.astype(BF16) for q in qs]
    a_mats = [jnp.where(strict, _mm(kb.astype(BF16), kh, _NT) * d, 0.0) for kb, kh, d in zip(kbs, khs, decays)]
    invs = _unit_lower_inverse(a_mats, tb)
    sols = [_mm3(inv, jnp.concatenate([v * b, kb * e_cum[:, ln:ln + 1]], axis=1))
            for inv, v, b, kb, ln in zip(invs, vs, betas, kbs, lanes)]
    u_bases = [sol[:, 0:DV_B] for sol in sols]
    whs = [sol[:, DV_B:].astype(BF16) for sol in sols]
    qkhs = [(_mm(qh, kh, _NT) * d).astype(BF16) for qh, kh, d in zip(qhs, khs, decays)]
    q_decs = [(q * e_cum[:, ln:ln + 1]).astype(BF16) for q, ln in zip(qs, lanes)]
    k_decs = [(k * e_rest[:, ln:ln + 1]).astype(BF16) for k, ln in zip(ks, lanes)]
    states = [s_scr[h] for h in heads]
    outs = [[] for _ in heads]
    for n in range(tb // DN_CHUNK):
        lo, hi = n * DN_CHUNK, (n + 1) * DN_CHUNK
        shs = [st.astype(BF16) for st in states]
        us = [ub[lo:hi] - _mm(wh[lo:hi], sh) for ub, wh, sh in zip(u_bases, whs, shs)]
        uhs = [u.astype(BF16) for u in us]
        for h in heads:
            outs[h].append(_mm(q_decs[h][lo:hi], shs[h]) + _mm(qkhs[h][lo:hi, lo:hi], uhs[h]))
        states = [st * e_tot[lo:lo + 1, ln:ln + 1] + _mm(kd[lo:hi], uh, _TN)
                  for st, ln, kd, uh in zip(states, lanes, k_decs, uhs)]
    for h in heads:
        s_scr[h] = states[h]
        o = jnp.concatenate(outs[h], axis=0)
        o = o * lax.rsqrt(jnp.mean(o * o, axis=-1, keepdims=True) + EPS) * gn_ref[...]
        o_ref[:, h * DV_B:(h + 1) * DV_B] = (o * _silu(z_ref[:, h * DV_B:(h + 1) * DV_B])).astype(BF16)

    @pl.when(ci == pl.num_programs(1) - 1)
    def _():
        s_ref[...] = s_scr[...]


def _dn_prompt(braw, bz, small, cw, par, gn, batch, seq):
    tb = DN_BLOCK
    nb = seq // tb
    row = lambda b, i: (b * nb + i, 0)
    const = lambda b, i: (0, 0)
    return pl.pallas_call(
        _dn_prompt_kernel,
        grid=(batch, nb),
        in_specs=[pl.BlockSpec((tb, CONV_CH), row), pl.BlockSpec((tb, W_B), row), pl.BlockSpec((tb, LANES), row),
                  pl.BlockSpec((CONV_W, CONV_CH), const), pl.BlockSpec((2, LANES), const),
                  pl.BlockSpec((1, DV_B), const)],
        out_specs=[pl.BlockSpec((tb, W_B), row),
                   pl.BlockSpec((None, H_B, DK_B, DV_B), lambda b, i: (b, 0, 0, 0))],
        out_shape=[jax.ShapeDtypeStruct((batch * seq, W_B), BF16),
                   jax.ShapeDtypeStruct((batch, H_B, DK_B, DV_B), F32)],
        scratch_shapes=[pltpu.VMEM((SUBLANES + tb, CONV_CH), F32), pltpu.VMEM((H_B, DK_B, DV_B), F32)],
        compiler_params=_cparams(("parallel", "arbitrary")),
        name="dn_prompt",
    )(braw, bz, small, cw, par, gn)


DN_DEC_SEQS = 8


def _dn_decode_kernel(raw_ref, z_ref, small_ref, conv_ref, st_ref, cw_ref, par_ref, gn_ref, o_ref, s_ref):
    nk = H_B * DK_B
    conv = raw_ref[...] * cw_ref[CONV_W - 1:CONV_W, :]
    for i in range(CONV_W - 1):
        conv = conv + conv_ref[:, i, :] * cw_ref[i:i + 1, :]
    qkv = _silu(conv)
    small = small_ref[...]
    beta_all = _sigmoid(small)
    eg_all = jnp.exp(-jnp.exp(par_ref[0:1, :]) * _softplus(small + par_ref[1:2, :]))
    zero7 = jnp.zeros((SUBLANES - 1, DK_B), F32)
    for h in range(H_B):
        q = _l2n(qkv[:, h * DK_B:(h + 1) * DK_B]) * (DK_B ** -0.5)
        k = _l2n(qkv[:, nk + h * DK_B:nk + (h + 1) * DK_B])
        v = qkv[:, 2 * nk + h * DV_B:2 * nk + (h + 1) * DV_B]
        beta = beta_all[:, SMALL_BETA + h:SMALL_BETA + h + 1]
        eg = eg_all[:, SMALL_DECAY + h:SMALL_DECAY + h + 1]
        qk = jnp.sum(q * k, axis=-1, keepdims=True)
        outs = []
        for s in range(DN_DEC_SEQS):
            state = st_ref[s, h]
            kq = jnp.concatenate([k[s:s + 1], q[s:s + 1], zero7[0:SUBLANES - 2]], axis=0).astype(BF16)
            proj = _mm(kq, state.astype(BF16))
            e = eg[s:s + 1]
            u = beta[s:s + 1] * (v[s:s + 1] - e * proj[0:1])
            outs.append(e * proj[1:2] + qk[s:s + 1] * u)
            k8 = jnp.concatenate([k[s:s + 1], zero7], axis=0).astype(BF16)
            u8 = jnp.concatenate([u, zero7], axis=0).astype(BF16)
            s_ref[s, h] = state * e + _mm(k8, u8, _TN)
        o = jnp.concatenate(outs, axis=0)
        o = o * lax.rsqrt(jnp.mean(o * o, axis=-1, keepdims=True) + EPS) * gn_ref[...]
        o_ref[:, h * DV_B:(h + 1) * DV_B] = (o * _silu(z_ref[:, h * DV_B:(h + 1) * DV_B])).astype(BF16)


def _dn_decode(braw, bz, small, state_conv, state_delta, cw, par, gn, layer):
    n_seq = braw.shape[0]
    ts = DN_DEC_SEQS
    row = lambda i: (i, 0)
    const = lambda i: (0, 0)
    return pl.pallas_call(
        _dn_decode_kernel,
        grid=(n_seq // ts,),
        in_specs=[pl.BlockSpec((ts, CONV_CH), row), pl.BlockSpec((ts, W_B), row), pl.BlockSpec((ts, LANES), row),
                  pl.BlockSpec((ts, None, CONV_W - 1, CONV_CH), lambda i: (i, layer, 0, 0)),
                  pl.BlockSpec((ts, None, H_B, DK_B, DV_B), lambda i: (i, layer, 0, 0, 0)),
                  pl.BlockSpec((CONV_W, CONV_CH), const), pl.BlockSpec((2, LANES), const),
                  pl.BlockSpec((1, DV_B), const)],
        out_specs=[pl.BlockSpec((ts, W_B), row),
                   pl.BlockSpec((ts, H_B, DK_B, DV_B), lambda i: (i, 0, 0, 0))],
        out_shape=[jax.ShapeDtypeStruct((n_seq, W_B), BF16),
                   jax.ShapeDtypeStruct((n_seq, H_B, DK_B, DV_B), F32)],
        compiler_params=_cparams(("parallel",)),
        name="dn_decode",
    )(braw, bz, small, state_conv, state_delta, cw, par, gn)


def _block_summaries(rows, kn0):
    n = rows.shape[0]
    m = jnp.mean(rows.reshape(n // CMP_BLOCK, CMP_BLOCK, LANES), axis=1)
    lane = lax.broadcasted_iota(jnp.int32, m.shape, 1)
    is_k = lane < DH_C
    ms = jnp.sum(jnp.where(is_k, m * m, 0.0), axis=-1, keepdims=True) * (1.0 / DH_C)
    return jnp.where(is_k, m * lax.rsqrt(ms + EPS) * kn0, m)


def _nsa_cmp_kernel(rows_ref, kn0_ref, o_ref):
    o_ref[...] = _block_summaries(rows_ref[...], kn0_ref[...]).astype(BF16)


def _nsa_cmp(nsa_rows, kn0, batch, seq):
    nb = seq // CMP_BLOCK
    return pl.pallas_call(
        _nsa_cmp_kernel,
        grid=(batch,),
        in_specs=[pl.BlockSpec((seq, LANES), lambda b: (b, 0)), pl.BlockSpec((1, LANES), lambda b: (0, 0))],
        out_specs=pl.BlockSpec((nb, LANES), lambda b: (b, 0)),
        out_shape=jax.ShapeDtypeStruct((batch * nb, LANES), BF16),
        compiler_params=_cparams(("parallel",)),
        name="nsa_cmp",
    )(nsa_rows, kn0)


def _alibi_slopes2(shape, axis, per_head):
    h = lax.broadcasted_iota(jnp.int32, shape, axis) // per_head
    return jnp.exp2((h + 1).astype(F32) * (-8.0 / H_C)) * LOG2E


def _masked_softmax2(s2, mask, axis):
    s2 = jnp.where(mask, s2, NEG_INF)
    m = jnp.max(s2, axis=axis, keepdims=True)
    e = jnp.where(mask, jnp.exp2(s2 - m), 0.0)
    return e / jnp.maximum(jnp.sum(e, axis=axis, keepdims=True), 1e-30)


def _nsa_prompt_kernel(qt_ref, small_ref, cmp_ref, kv_ref, kvt_ref, gn_ref, o_ref, s_scr, *, n_blocks):
    i = pl.program_id(1)
    nr = H_C * TQ
    tk = NSA_TK
    qt = jnp.concatenate([qt_ref[h * DH_C:(h + 1) * DH_C, :] for h in range(H_C)], axis=1)
    qt_pad = jnp.concatenate([qt, jnp.zeros_like(qt)], axis=0)
    t_col = i * TQ + lax.broadcasted_iota(jnp.int32, (1, nr), 1) % TQ
    t_one = t_col[:, 0:TQ]
    slope_c = _alibi_slopes2((1, nr), 1, TQ)

    kcb = cmp_ref[:, 0:DH_C]
    vcb = cmp_ref[:, DH_C:2 * DH_C]
    blk = lax.broadcasted_iota(jnp.int32, (n_blocks, 1), 0)
    blk_mid = (blk * CMP_BLOCK).astype(F32) + 0.5 * (CMP_BLOCK - 1)
    vis_c = ((blk + 1) * CMP_BLOCK - 1) <= t_col
    s_c = _mm(kcb, qt) - slope_c * (t_col.astype(F32) - blk_mid)
    p_c = _masked_softmax2(s_c, vis_c, 0)
    o_c = _mm(vcb, p_c.astype(BF16), _TN)

    imp = p_c[:, 0:TQ]
    for h in range(1, H_C):
        imp = imp + p_c[:, h * TQ:(h + 1) * TQ]
    cur = t_one // SEL_BLOCK
    jb = lax.broadcasted_iota(jnp.int32, (n_blocks, TQ), 0)
    forced = (jb == 0) | (jb == cur) | (jb == cur - 1)
    score = jnp.where(jb > cur, -1.0, jnp.where(forced, 2.0 * H_C, imp))
    rank = jnp.zeros((n_blocks, TQ), F32)
    for n in range(n_blocks):
        other = score[n:n + 1, :]
        rank = rank + ((other > score) | ((other == score) & (jb > n))).astype(F32)
    sel = ((rank < float(min(TOP_K_BLOCKS, n_blocks))) & (jb <= cur)).astype(BF16)

    bpt = tk // SEL_BLOCK
    er = lax.broadcasted_iota(jnp.int32, (tk, n_blocks), 0) // SEL_BLOCK
    ec = lax.broadcasted_iota(jnp.int32, (tk, n_blocks), 1)
    s_off = lax.broadcasted_iota(jnp.int32, (tk, 1), 0)
    row = lax.broadcasted_iota(jnp.int32, (LANES, tk), 0)

    def attend(j_lo, j_hi, kv_lo, selected):
        def score_tile(j):
            start = pl.multiple_of(j * tk, tk)
            kv = kv_ref[pl.ds(start, tk), kv_lo:kv_lo + LANES]
            s_pos = start + s_off
            valid = s_pos <= t_one
            if selected:
                valid = valid & (_mm((er + j * bpt == ec).astype(BF16), sel) > 0.5)
            else:
                valid = valid & (t_one - s_pos < WINDOW)
            dist = jnp.where(valid, (s_pos - t_one).astype(F32), NEG_INF)
            s2 = _mm(kv, qt_pad) + slope_c * jnp.concatenate([dist] * H_C, axis=1)
            s_scr[j] = s2
            return jnp.max(s2.reshape(tk // SUBLANES, SUBLANES, nr), axis=0)

        p_lo, p_hi = j_lo // 2, (j_hi + 1) // 2

        def scores(p, mx):
            return jnp.maximum(mx, jnp.maximum(score_tile(2 * p), score_tile(2 * p + 1)))

        mx = lax.fori_loop(p_lo, p_hi, scores, jnp.full((SUBLANES, nr), NEG_INF, F32))
        m = jnp.max(mx, axis=0, keepdims=True)

        def weigh_tile(j):
            start = pl.multiple_of(j * tk, tk)
            kvt = kvt_ref[kv_lo:kv_lo + LANES, pl.ds(start, tk)]
            ones_vt = jnp.where(row < DH_C, 1.0, kvt)
            p = jnp.exp2(s_scr[j] - m)
            return _mm(ones_vt, p.astype(BF16))

        def weigh(p, acc):
            return acc + (weigh_tile(2 * p) + weigh_tile(2 * p + 1))

        acc = lax.fori_loop(p_lo, p_hi, weigh, jnp.zeros((LANES, nr), F32))
        return acc[DH_C:2 * DH_C] / acc[0:DH_C]

    j_end = (i * TQ) // tk + 1
    o_s = attend(0, j_end, 0, True)
    j_win = jnp.maximum(i * TQ - (WINDOW - 1), 0) // tk
    o_w = attend(j_win, j_end, 2 * DH_C, False)

    small_t = small_ref[...].T

    def gate_row(rr):
        g = jnp.concatenate([small_t[SMALL_GATE + 3 * h + rr:SMALL_GATE + 3 * h + rr + 1, :]
                             for h in range(H_C)], axis=1)
        return _sigmoid(g)

    o = gate_row(0) * o_c + gate_row(1) * o_s + gate_row(2) * o_w
    o = o * lax.rsqrt(jnp.mean(o * o, axis=0, keepdims=True) + EPS)
    gain = jnp.concatenate([jnp.broadcast_to(gn_ref[:, h:h + 1], (DH_C, TQ)) for h in range(H_C)], axis=1)
    o = o * gain
    pairs = [jnp.concatenate([o[:, (2 * g) * TQ:(2 * g + 1) * TQ], o[:, (2 * g + 1) * TQ:(2 * g + 2) * TQ]], axis=0).T
             for g in range(H_C // 2)]
    o_ref[...] = jnp.concatenate(pairs, axis=1).astype(BF16)


def _nsa_prompt(cqt, small, cmp, ckv, ckvt, gn_cols, batch, seq):
    assert seq % (2 * NSA_TK) == 0, "key tiles are processed in pairs"
    nq = seq // TQ
    nb = seq // CMP_BLOCK
    row = lambda b, i: (b * nq + i, 0)
    kern = functools.partial(_nsa_prompt_kernel, n_blocks=nb)
    return pl.pallas_call(
        kern,
        grid=(batch, nq),
        in_specs=[pl.BlockSpec((W_C, TQ), lambda b, i: (0, b * nq + i)), pl.BlockSpec((TQ, LANES), row),
                  pl.BlockSpec((nb, LANES), lambda b, i: (b, 0)),
                  pl.BlockSpec((seq, 4 * DH_C), lambda b, i: (b, 0)),
                  pl.BlockSpec((4 * DH_C, seq), lambda b, i: (0, b)),
                  pl.BlockSpec((DH_C, H_C), lambda b, i: (0, 0))],
        out_specs=pl.BlockSpec((TQ, W_C), row),
        out_shape=jax.ShapeDtypeStruct((batch * seq, W_C), BF16),
        scratch_shapes=[pltpu.VMEM((seq // NSA_TK, NSA_TK, H_C * TQ), F32)],
        compiler_params=_cparams(("parallel", "arbitrary")),
        name="nsa_prompt",
    )(cqt, small, cmp, ckv, ckvt, gn_cols)


def _nsa_decode_kernel(pt_ref, q_ref, small_ref, cur_nsa_ref, cur_win_ref, win_ref, kn0_ref, gn_ref, member_ref,
                       cache_ref, o_ref, buf_ref, sem_ref, *, layer, n_pages, page):
    slot = _paged_prefetch(pt_ref, cache_ref, buf_ref, sem_ref, layer, n_pages, page)
    group = buf_ref.shape[1]
    _interleave([_nsa_decode_one(pl.program_id(0) * group + g, buf_ref.at[slot, g], win_ref.at[g], q_ref,
                                 small_ref, cur_nsa_ref, cur_win_ref, kn0_ref, gn_ref, member_ref, o_ref,
                                 n_pages * page) for g in range(group)])


def _nsa_decode_one(s, past_ref, win_ref, q_ref, small_ref, cur_nsa_ref, cur_win_ref, kn0_ref, gn_ref, member_ref,
                    o_ref, n_past):
    n_cmp = n_past // CMP_BLOCK
    cur = n_past // SEL_BLOCK
    t_f = float(n_past)
    hr = SUBLANES

    qrow = q_ref[pl.ds(s, 1), :]
    q8 = jnp.concatenate([qrow[:, h * DH_C:(h + 1) * DH_C] for h in range(H_C)]
                         + [jnp.zeros((hr - H_C, DH_C), F32)], axis=0).astype(BF16)
    slope = _alibi_slopes2((hr, 1), 0, 1)

    summ = _mm(past_ref[0:2 * DH_C, :].astype(BF16), member_ref[...], _NT)[:, 0:n_cmp] * (1.0 / CMP_BLOCK)
    yield
    km = summ[0:DH_C]
    ms = jnp.sum(km * km, axis=0, keepdims=True) * (1.0 / DH_C)
    kcb = (km * lax.rsqrt(ms + EPS) * kn0_ref[...]).astype(BF16)
    vcb = summ[DH_C:2 * DH_C].astype(BF16)
    blk = lax.broadcasted_iota(jnp.int32, (1, n_cmp), 1)
    blk_mid = (blk * CMP_BLOCK).astype(F32) + 0.5 * (CMP_BLOCK - 1)
    s_c = _mm(q8, kcb) - slope * (t_f - blk_mid)
    yield
    p_c = _masked_softmax2(s_c, blk >= 0, -1)
    o_c = _mm(p_c.astype(BF16), vcb, _NT)

    hrow = lax.broadcasted_iota(jnp.int32, (hr, n_cmp), 0)
    imp = jnp.sum(jnp.where(hrow < H_C, p_c, 0.0), axis=0, keepdims=True)
    imp = jnp.concatenate([imp, jnp.zeros((1, LANES - n_cmp), F32)], axis=1)
    jb = lax.broadcasted_iota(jnp.int32, (1, LANES), 1)
    forced = (jb == 0) | (jb == cur) | (jb == cur - 1)
    score = jnp.where(jb > cur, -1.0, jnp.where(forced, 2.0 * H_C, imp))
    srow = jnp.broadcast_to(score, (LANES, LANES))
    scol = srow.T
    ri = lax.broadcasted_iota(jnp.int32, (LANES, LANES), 0)
    ci = lax.broadcasted_iota(jnp.int32, (LANES, LANES), 1)
    ahead = (scol > srow) | ((scol == srow) & (ri < ci))
    rank = jnp.sum(ahead.astype(F32), axis=0, keepdims=True)
    sel = ((rank < float(min(TOP_K_BLOCKS, cur + 1))) & (jb <= cur)).astype(BF16)
    yield
    sel8 = jnp.broadcast_to(sel, (hr, LANES))
    picked = _mm(sel8, member_ref[...]) > 0.5

    ks = past_ref[2 * DH_C:3 * DH_C, :].astype(BF16)
    vs = past_ref[3 * DH_C:4 * DH_C, :].astype(BF16)
    pos = lax.broadcasted_iota(jnp.int32, (1, n_past), 1).astype(F32)
    s_s = _mm(q8, ks) - slope * (t_f - pos)
    cur_row = cur_nsa_ref[pl.ds(s, 1), :]
    q8f = q8.astype(F32)
    s_cur = jnp.sum(q8f * cur_row[:, 2 * DH_C:3 * DH_C].astype(BF16).astype(F32), axis=-1, keepdims=True)
    yield
    s_s = jnp.where(picked, s_s, NEG_INF)
    m = jnp.maximum(jnp.max(s_s, axis=-1, keepdims=True), s_cur)
    e = jnp.where(picked, jnp.exp2(s_s - m), 0.0)
    e_cur = jnp.exp2(s_cur - m)
    den = jnp.sum(e, axis=-1, keepdims=True) + e_cur
    o_s = (_mm(e.astype(BF16), vs, _NT) + e_cur * cur_row[:, 3 * DH_C:4 * DH_C]) / den

    yield
    n_buf = win_ref.shape[1]
    kw = win_ref[0:DH_C, :].astype(BF16)
    vw = win_ref[DH_C:2 * DH_C, :].astype(BF16)
    wpos = n_past - n_buf + lax.broadcasted_iota(jnp.int32, (1, n_buf), 1)
    vis_w = (n_past - wpos < WINDOW) & (wpos >= 0)
    s_w = _mm(q8, kw) - slope * (t_f - wpos.astype(F32))
    cur_w = cur_win_ref[pl.ds(s, 1), :]
    sw_cur = jnp.sum(q8f * cur_w[:, 0:DH_C].astype(BF16).astype(F32), axis=-1, keepdims=True)
    yield
    s_w = jnp.where(vis_w, s_w, NEG_INF)
    m = jnp.maximum(jnp.max(s_w, axis=-1, keepdims=True), sw_cur)
    e = jnp.where(vis_w, jnp.exp2(s_w - m), 0.0)
    e_cur = jnp.exp2(sw_cur - m)
    den = jnp.sum(e, axis=-1, keepdims=True) + e_cur
    o_w = (_mm(e.astype(BF16), vw, _NT) + e_cur * cur_w[:, DH_C:2 * DH_C]) / den

    yield
    gates = jnp.broadcast_to(_sigmoid(small_ref[pl.ds(s, 1), :]), (hr, LANES))
    gl = lax.broadcasted_iota(jnp.int32, (hr, LANES), 1)
    gh = lax.broadcasted_iota(jnp.int32, (hr, LANES), 0)

    def gate_col(rr):
        return jnp.sum(jnp.where(gl == SMALL_GATE + 3 * gh + rr, gates, 0.0), axis=-1, keepdims=True)

    o = gate_col(0) * o_c + gate_col(1) * o_s + gate_col(2) * o_w
    o = o * lax.rsqrt(jnp.mean(o * o, axis=-1, keepdims=True) + EPS)
    orow = jnp.concatenate([o[h:h + 1] for h in range(H_C)], axis=1)
    o_ref[pl.ds(s, 1), :] = orow * gn_ref[...]


def _nsa_decode(page_table, cq, small, cur_nsa, cur_win, state_win, kn0, gn, cache, layer):
    n_seq, n_pages = page_table.shape
    page = cache.shape[3]
    n_buf = state_win.shape[3]
    n_past = n_pages * page
    assert CMP_BLOCK == SEL_BLOCK and n_past // SEL_BLOCK < LANES
    member = (lax.broadcasted_iota(jnp.int32, (LANES, n_past), 0)
              == lax.broadcasted_iota(jnp.int32, (LANES, n_past), 1) // SEL_BLOCK).astype(BF16)
    kern = functools.partial(_nsa_decode_kernel, layer=layer, n_pages=n_pages, page=page)
    whole = lambda shape: pl.BlockSpec(shape, lambda s, pt: (0,) * len(shape))
    return pl.pallas_call(
        kern,
        grid_spec=pltpu.PrefetchScalarGridSpec(
            num_scalar_prefetch=1,
            grid=(n_seq // NSA_DEC_GROUP,),
            in_specs=[whole((n_seq, W_C)), whole((n_seq, LANES)), whole((n_seq, 4 * DH_C)), whole((n_seq, 2 * DH_C)),
                      pl.BlockSpec((NSA_DEC_GROUP, None, 2 * DH_C, n_buf), lambda s, pt: (s, layer, 0, 0)),
                      whole((DH_C, 1)), whole((1, W_C)), whole((LANES, n_past)),
                      pl.BlockSpec(memory_space=pl.ANY)],
            out_specs=whole((n_seq, W_C)),
            scratch_shapes=[pltpu.VMEM((2, NSA_DEC_GROUP, 4 * DH_C, n_pages * page), F32),
                            pltpu.SemaphoreType.DMA((2,))]),
        out_shape=jax.ShapeDtypeStruct((n_seq, W_C), F32),
        compiler_params=_cparams(("arbitrary",)),
        name="nsa_decode",
    )(page_table, cq, small, cur_nsa, cur_win, state_win, kn0, gn, member, cache)


def _pack_w_in(w_in, layer):
    wt = jnp.transpose(w_in, (2, 0, 1))[:, layer, :]
    widths = [W_A, W_A, W_A, H_B * DK_B, H_B * DK_B, W_B, W_B, H_B, H_B, W_C] + [DH_C] * 6 + [3 * H_C]
    rows, start = [], 0
    for w in widths:
        rows.append(wt[start:start + w])
        start += w
    (a_q, a_k, a_v, b_q, b_k, b_v, b_z, b_beta, b_decay, c_q, c_kc, c_vc, c_ks, c_vs, c_kw, c_vw, c_gate) = rows
    pad = jnp.zeros((LANES - 2 * H_B - 3 * H_C, wt.shape[1]), wt.dtype)
    packed = jnp.concatenate([a_k, a_v, a_q, b_q, b_k, b_v, b_z, c_q, c_kc, c_vc, c_ks, c_vs, c_kw, c_vw,
                              b_beta, b_decay, c_gate, pad], axis=0)
    return packed.astype(BF16)


def _row_tile(t, cap):
    tm = min(t, cap)
    while t % tm:
        tm //= 2
    return tm


def kernel(x_prompt, x_sample, cache_sb, cache_nsa, state_win, state_delta, state_conv, page_table, w_in, conv_w,
           a_log, dt_bias, b_out_norm, a_out_norm, c_q_norm, c_k_norm, c_out_norm, w_out, attn_norm, ffn_norm,
           dense_w_gu, dense_w_down, router_w, router_b, moe_w_gu, moe_w_down):
    batch, seq, _ = x_prompt.shape
    n_dec = x_sample.shape[0]
    depth = w_in.shape[0]
    n_phys, _, page = cache_sb.shape[:3]
    n_buf = state_win.shape[2]
    tp, ts = batch * seq, n_dec
    tm_p, tm_s = _row_tile(tp, 512), _row_tile(ts, 512)

    cache_sb2 = jnp.transpose(cache_sb, (0, 1, 3, 4, 5, 2)).reshape(n_phys, depth, 2 * W_A, page)
    cache_nsa2 = jnp.transpose(cache_nsa, (0, 1, 3, 4, 2)).reshape(n_phys, depth, 4 * DH_C, page)
    state_win2 = jnp.transpose(state_win, (0, 1, 3, 4, 2)).reshape(n_dec, depth, 2 * DH_C, n_buf)
    d_ff = dense_w_down.shape[1]
    tf = d_ff // 2 if (d_ff // 2) % LANES == 0 else d_ff

    xp = x_prompt.reshape(tp, D_MODEL)
    xs = x_sample.reshape(ts, D_MODEL)
    outs_p = {k: [] for k in ("sb", "nsa", "win", "delta", "conv")}
    outs_s = {k: [] for k in ("sb", "nsa", "win", "delta", "conv")}
    for l in range(depth):
        w_pk = _pack_w_in(w_in, l)
        wo = w_out[l].astype(BF16)
        g_attn = attn_norm[l].reshape(1, D_MODEL)
        g_ffn = ffn_norm[l].reshape(1, D_MODEL)
        qn = jnp.tile(c_q_norm[l], H_C).reshape(1, W_C)
        ones64 = jnp.ones((DH_C,), F32)
        kn = jnp.concatenate([ones64, ones64, c_k_norm[l, 1], ones64, c_k_norm[l, 2], ones64]).reshape(1, 384)
        kn0 = jnp.concatenate([c_k_norm[l, 0], ones64]).reshape(1, LANES)
        par = (jnp.zeros((2, LANES), F32).at[0, SMALL_DECAY:SMALL_DECAY + H_B].set(a_log[l])
               .at[1, SMALL_DECAY:SMALL_DECAY + H_B].set(dt_bias[l]))
        gn_a = a_out_norm[l].reshape(1, W_A)
        gn_b = b_out_norm[l].reshape(1, DV_B)
        gn_c = c_out_norm[l].reshape(1, W_C)
        cw = conv_w[l]

        sb, aq, akv, braw, bz, small, _, nsa, win, ckv, cqt, ckvt = _inproj(xp, g_attn, w_pk, qn, kn, tm_p)
        oa = _sb_prompt(aq, akv, gn_a, batch, seq)
        ob, s_fin = _dn_prompt(braw, bz, small, cw, par, gn_b, batch, seq)
        cmp = _nsa_cmp(nsa, kn0, batch, seq)
        oc = _nsa_prompt(cqt, small, cmp, ckv, ckvt, c_out_norm[l].reshape(H_C, DH_C).T, batch, seq)
        outs_p["sb"].append(sb.reshape(batch, seq, 2, H_A, DH_A))
        outs_p["nsa"].append(nsa.reshape(batch, seq, 4, DH_C))
        outs_p["win"].append(win.reshape(batch, seq, 2, DH_C)[:, seq - min(WINDOW, seq):])
        outs_p["delta"].append(s_fin)
        outs_p["conv"].append(braw.reshape(batch, seq, CONV_CH)[:, seq - (CONV_W - 1):])

        sb_s, aq_s, _, braw_s, bz_s, small_s, cq_s, nsa_s, win_s, _, _, _ = _inproj(xs, g_attn, w_pk, qn, kn, tm_s)
        oa_s = _sb_decode(page_table, aq_s.astype(F32), gn_a, cache_sb2, l).astype(BF16)
        ob_s, s_new = _dn_decode(braw_s, bz_s, small_s, state_conv, state_delta, cw, par, gn_b, l)
        oc_s = _nsa_decode(page_table, cq_s.astype(F32), small_s, nsa_s, win_s, state_win2,
                           c_k_norm[l, 0].reshape(DH_C, 1), gn_c, cache_nsa2, l).astype(BF16)
        outs_s["sb"].append(sb_s.reshape(n_dec, 1, 2, H_A, DH_A))
        outs_s["nsa"].append(nsa_s.reshape(n_dec, 1, 4, DH_C))
        win_all = jnp.concatenate([state_win[:, l], win_s.reshape(n_dec, 1, 2, DH_C)], axis=1)
        outs_s["win"].append(win_all[:, win_all.shape[1] - n_buf:])
        outs_s["delta"].append(s_new)
        outs_s["conv"].append(jnp.concatenate([state_conv[:, l], braw_s.reshape(n_dec, 1, CONV_CH)], axis=1)[:, 1:])

        if l % 2 == 0:
            wgu = dense_w_gu[l // 2].astype(BF16)
            wd = dense_w_down[l // 2].astype(BF16)
            xp = _ffn_dense(xp, oa, ob, oc, wo, g_ffn, wgu, wd, tm_p, tf)
            xs = _ffn_dense(xs, oa_s, ob_s, oc_s, wo, g_ffn, wgu, wd, tm_s, tf)
        else:
            wr = jnp.pad(router_w[l // 2], ((0, 0), (0, LANES - N_EXPERTS)))
            br = jnp.pad(router_b[l // 2], (0, LANES - N_EXPERTS)).reshape(1, LANES)
            wgu = moe_w_gu[l // 2].astype(BF16)
            wd = moe_w_down[l // 2].astype(BF16)
            xp = _ffn_moe(xp, oa, ob, oc, wo, g_ffn, wr, br, wgu, wd, tm_p)
            xs = _ffn_moe(xs, oa_s, ob_s, oc_s, wo, g_ffn, wr, br, wgu, wd, tm_s)

    stack = lambda xs_: jnp.stack(xs_, axis=1)
    return (xp.reshape(batch, seq, D_MODEL), xs.reshape(n_dec, 1, D_MODEL),
            stack(outs_p["sb"]), stack(outs_s["sb"]), stack(outs_p["nsa"]), stack(outs_s["nsa"]),
            stack(outs_p["win"]), stack(outs_s["win"]), stack(outs_p["delta"]), stack(outs_s["delta"]),
            stack(outs_p["conv"]), stack(outs_s["conv"]))
```

```python
import functools

import jax
import jax.numpy as jnp
from jax import lax
from jax.experimental import pallas as pl
from jax.experimental.pallas import tpu as pltpu

F32 = jnp.float32
BF16 = jnp.bfloat16

D_MODEL = 1024
H_A, DH_A = 4, 64
H_B, DK_B, DV_B = 4, 128, 128
H_C, DH_C = 4, 64
W_A = H_A * DH_A
W_B = H_B * DV_B
W_C = H_C * DH_C
CONV_W = 4
CONV_CH = 2 * H_B * DK_B + W_B
DN_CHUNK = 64
CMP_BLOCK = 64
SEL_BLOCK = 64
TOP_K_BLOCKS = 16
WINDOW = 512
N_EXPERTS = 8
EPS = 1e-6
NEG_INF = -1e30

LANES = 128
SUBLANES = 8
VMEM_LIMIT = 56 * 1024 * 1024

N_PROJ = 512 + 256 + CONV_CH + W_B + W_C + 256 + 128 + 128
SMALL_BETA, SMALL_DECAY, SMALL_GATE = 0, H_B, 2 * H_B
TQ = 128
SB_TK = 256
NSA_TK = 256
LOG2E = 1.4426950408889634
DN_BLOCK = 256


def _cparams(sem):
    return pltpu.CompilerParams(dimension_semantics=sem, vmem_limit_bytes=VMEM_LIMIT)


def _split2(x):
    hi = x.astype(BF16)
    lo = (x - hi.astype(F32)).astype(BF16)
    return hi, lo


def _split3(x):
    hi = x.astype(BF16)
    r = x - hi.astype(F32)
    mid = r.astype(BF16)
    lo = (r - mid.astype(F32)).astype(BF16)
    return hi, mid, lo


_NN = (((1,), (0,)), ((), ()))
_NT = (((1,), (1,)), ((), ()))
_TN = (((0,), (0,)), ((), ()))


def _mm(a, b, dims=_NN):
    return lax.dot_general(a, b, dims, preferred_element_type=F32)


def _mm_x01(x, m01, parts=2):
    ps = _split2(x) if parts == 2 else _split3(x)
    out = _mm(ps[0], m01)
    for p in ps[1:]:
        out = out + _mm(p, m01)
    return out


def _mm_01x(m01, x, parts=2):
    ps = _split2(x) if parts == 2 else _split3(x)
    out = _mm(m01, ps[0])
    for p in ps[1:]:
        out = out + _mm(m01, p)
    return out


def _mm3(a, b, dims=_NN):
    ah, al = _split2(a)
    bh, bl = _split2(b)
    return _mm(ah, bh, dims) + (_mm(ah, bl, dims) + _mm(al, bh, dims))


def _silu(x):
    return x * (1.0 / (1.0 + jnp.exp(-x)))


def _sigmoid(x):
    return 1.0 / (1.0 + jnp.exp(-x))


def _softplus(x):
    return jnp.maximum(x, 0.0) + jnp.log(1.0 + jnp.exp(-jnp.abs(x)))


def _group_ones(n, group):
    r = lax.broadcasted_iota(jnp.int32, (n, n), 0) // group
    c = lax.broadcasted_iota(jnp.int32, (n, n), 1) // group
    return (r == c).astype(BF16)


def _inproj_kernel(x_ref, g_ref, w_ref, qn_ref, kn_ref,
                   sb_ref, aq_ref, akv_ref, braw_ref, bz_ref, small_ref, cq_ref, nsa_ref, win_ref, ckv_ref,
                   cqt_ref, ckvt_ref):
    x = x_ref[...]
    h = x * lax.rsqrt(jnp.mean(x * x, axis=-1, keepdims=True) + EPS) * g_ref[...]
    p = _mm(h.astype(BF16), w_ref[...], _NT)
    o = 0
    sb = p[:, o:o + 512]; o += 512
    aq = p[:, o:o + 256]; o += 256
    braw = p[:, o:o + CONV_CH]; o += CONV_CH
    bz = p[:, o:o + W_B]; o += W_B
    cq = p[:, o:o + W_C]; o += W_C
    nsa = p[:, o:o + 256]; o += 256
    win = p[:, o:o + 128]; o += 128
    small = p[:, o:o + 128]
    sb_ref[...] = sb
    aq_ref[...] = (aq * (DH_A ** -0.5 * LOG2E)).astype(BF16)
    akv_ref[...] = sb.astype(BF16)
    braw_ref[...] = braw
    bz_ref[...] = bz
    small_ref[...] = small
    g64 = _group_ones(256, DH_C)
    ms = _mm_x01(cq * cq, g64) * (1.0 / DH_C)
    cqn = cq * lax.rsqrt(ms + EPS) * qn_ref[...]
    cq_ref[...] = (cqn * (DH_C ** -0.5 * LOG2E)).astype(BF16)
    lane = lax.broadcasted_iota(jnp.int32, nsa.shape, 1)
    is_ks = (lane >= 128) & (lane < 192)
    ms = _mm_x01(nsa * nsa, g64) * (1.0 / DH_C)
    nsa_n = jnp.where(is_ks, nsa * lax.rsqrt(ms + EPS) * kn_ref[:, 0:256], nsa)
    nsa_ref[...] = nsa_n
    lane = lax.broadcasted_iota(jnp.int32, win.shape, 1)
    ms = _mm_x01(win * win, g64[0:128, 0:128]) * (1.0 / DH_C)
    win_n = jnp.where(lane < 64, win * lax.rsqrt(ms + EPS) * kn_ref[:, 256:384], win)
    win_ref[...] = win_n
    ckv = jnp.concatenate([nsa_n[:, 128:256], win_n], axis=1)
    ckv_ref[...] = ckv.astype(BF16)
    cqt_ref[...] = (cqn * (DH_C ** -0.5 * LOG2E)).T.astype(BF16)
    ckvt_ref[...] = ckv.T.astype(BF16)


def _inproj(x, g, w, qn, kn, tm):
    t = x.shape[0]
    row = lambda i: (i, 0)
    const = lambda i: (0, 0)
    outs = [(512, F32), (256, BF16), (512, BF16), (CONV_CH, F32), (W_B, F32), (128, F32),
            (W_C, BF16), (256, F32), (128, F32), (256, BF16)]
    return pl.pallas_call(
        _inproj_kernel,
        grid=(t // tm,),
        in_specs=[pl.BlockSpec((tm, D_MODEL), row), pl.BlockSpec((1, D_MODEL), const),
                  pl.BlockSpec((N_PROJ, D_MODEL), const), pl.BlockSpec((1, 256), const),
                  pl.BlockSpec((1, 384), const)],
        out_specs=[pl.BlockSpec((tm, n), row) for n, _ in outs]
        + [pl.BlockSpec((W_C, tm), lambda i: (0, i)), pl.BlockSpec((4 * DH_C, tm), lambda i: (0, i))],
        out_shape=[jax.ShapeDtypeStruct((t, n), d) for n, d in outs]
        + [jax.ShapeDtypeStruct((W_C, t), BF16), jax.ShapeDtypeStruct((4 * DH_C, t), BF16)],
        compiler_params=_cparams(("parallel",)),
        name="inproj",
    )(x, g, w, qn, kn)


def _mix_prologue(x_ref, oa_ref, ob_ref, oc_ref, wo_ref, g_ref, xn_scr, h_scr):
    mix = (_mm(oa_ref[...], wo_ref[0:W_A, :]) + _mm(ob_ref[...], wo_ref[W_A:W_A + W_B, :])
           + _mm(oc_ref[...], wo_ref[W_A + W_B:, :]))
    xn = x_ref[...] + mix
    xn_scr[...] = xn
    h = xn * lax.rsqrt(jnp.mean(xn * xn, axis=-1, keepdims=True) + EPS) * g_ref[...]
    h_scr[...] = h.astype(BF16)
    return h


def _ffn_dense_kernel(x_ref, oa_ref, ob_ref, oc_ref, wo_ref, g_ref, wg_ref, wu_ref, wd_ref, o_ref,
                      xn_scr, h_scr, acc_scr):
    f = pl.program_id(1)

    @pl.when(f == 0)
    def _():
        _mix_prologue(x_ref, oa_ref, ob_ref, oc_ref, wo_ref, g_ref, xn_scr, h_scr)
        acc_scr[...] = jnp.zeros_like(acc_scr)

    h = h_scr[...]
    act = _silu(_mm(h, wg_ref[...])) * _mm(h, wu_ref[...])
    acc_scr[...] += _mm(act.astype(BF16), wd_ref[...])

    @pl.when(f == pl.num_programs(1) - 1)
    def _():
        o_ref[...] = xn_scr[...] + acc_scr[...]


def _ffn_dense(x, oa, ob, oc, wo, g, wgu, wd, tm, tf):
    t = x.shape[0]
    d_ff = wd.shape[0]
    nf = d_ff // tf
    row = lambda i, f: (i, 0)
    const = lambda i, f: (0, 0)
    return pl.pallas_call(
        _ffn_dense_kernel,
        grid=(t // tm, nf),
        in_specs=[pl.BlockSpec((tm, D_MODEL), row), pl.BlockSpec((tm, W_A), row), pl.BlockSpec((tm, W_B), row),
                  pl.BlockSpec((tm, W_C), row), pl.BlockSpec((D_MODEL, D_MODEL), const),
                  pl.BlockSpec((1, D_MODEL), const),
                  pl.BlockSpec((D_MODEL, tf), lambda i, f: (0, f)),
                  pl.BlockSpec((D_MODEL, tf), lambda i, f: (0, f + nf)),
                  pl.BlockSpec((tf, D_MODEL), lambda i, f: (f, 0))],
        out_specs=pl.BlockSpec((tm, D_MODEL), row),
        out_shape=jax.ShapeDtypeStruct((t, D_MODEL), F32),
        scratch_shapes=[pltpu.VMEM((tm, D_MODEL), F32), pltpu.VMEM((tm, D_MODEL), BF16),
                        pltpu.VMEM((tm, D_MODEL), F32)],
        compiler_params=_cparams(("parallel", "arbitrary")),
        name="ffn_dense",
    )(x, oa, ob, oc, wo, g, wgu, wgu, wd)


def _ffn_moe_kernel(x_ref, oa_ref, ob_ref, oc_ref, wo_ref, g_ref, wr_ref, br_ref, wg_ref, wu_ref, wd_ref, o_ref,
                    xn_scr, h_scr, acc_scr, gate_scr):
    e = pl.program_id(1)

    @pl.when(e == 0)
    def _():
        h = _mix_prologue(x_ref, oa_ref, ob_ref, oc_ref, wo_ref, g_ref, xn_scr, h_scr)
        acc_scr[...] = jnp.zeros_like(acc_scr)
        logits = _mm3(h, wr_ref[...]) + br_ref[...]
        lane = lax.broadcasted_iota(jnp.int32, logits.shape, 1)
        logits = jnp.where(lane < N_EXPERTS, logits, -jnp.inf)
        m1 = jnp.max(logits, axis=-1, keepdims=True)
        i1 = jnp.min(jnp.where(logits == m1, lane, LANES), axis=-1, keepdims=True)
        rest = jnp.where(lane == i1, -jnp.inf, logits)
        m2 = jnp.max(rest, axis=-1, keepdims=True)
        i2 = jnp.min(jnp.where(rest == m2, lane, LANES), axis=-1, keepdims=True)
        e2 = jnp.exp(m2 - m1)
        den = 1.0 + e2
        gate_scr[...] = jnp.where(lane == i1, 1.0 / den, 0.0) + jnp.where(lane == i2, e2 / den, 0.0)

    h = h_scr[...]
    act = _silu(_mm(h, wg_ref[...])) * _mm(h, wu_ref[...])
    y = _mm(act.astype(BF16), wd_ref[...])
    lane = lax.broadcasted_iota(jnp.int32, gate_scr.shape, 1)
    ge = jnp.sum(jnp.where(lane == e, gate_scr[...], 0.0), axis=-1, keepdims=True)
    acc_scr[...] += ge * y

    @pl.when(e == pl.num_programs(1) - 1)
    def _():
        o_ref[...] = xn_scr[...] + acc_scr[...]


def _ffn_moe(x, oa, ob, oc, wo, g, wr, br, wgu, wd, tm):
    t = x.shape[0]
    ne, _, two_f = wgu.shape
    f = two_f // 2
    row = lambda i, e: (i, 0)
    const = lambda i, e: (0, 0)
    return pl.pallas_call(
        _ffn_moe_kernel,
        grid=(t // tm, ne),
        in_specs=[pl.BlockSpec((tm, D_MODEL), row), pl.BlockSpec((tm, W_A), row), pl.BlockSpec((tm, W_B), row),
                  pl.BlockSpec((tm, W_C), row), pl.BlockSpec((D_MODEL, D_MODEL), const),
                  pl.BlockSpec((1, D_MODEL), const), pl.BlockSpec((D_MODEL, LANES), const),
                  pl.BlockSpec((1, LANES), const),
                  pl.BlockSpec((None, D_MODEL, f), lambda i, e: (e, 0, 0)),
                  pl.BlockSpec((None, D_MODEL, f), lambda i, e: (e, 0, 1)),
                  pl.BlockSpec((None, f, D_MODEL), lambda i, e: (e, 0, 0))],
        out_specs=pl.BlockSpec((tm, D_MODEL), row),
        out_shape=jax.ShapeDtypeStruct((t, D_MODEL), F32),
        scratch_shapes=[pltpu.VMEM((tm, D_MODEL), F32), pltpu.VMEM((tm, D_MODEL), BF16),
                        pltpu.VMEM((tm, D_MODEL), F32), pltpu.VMEM((tm, LANES), F32)],
        compiler_params=_cparams(("parallel", "arbitrary")),
        name="ffn_moe",
    )(x, oa, ob, oc, wo, g, wr, br, wgu, wgu, wd)


def _sb_log2_terms(z2):
    sp = jnp.log2(1.0 + jnp.exp2(-jnp.abs(z2)))
    ls = jnp.minimum(z2, 0.0) - sp
    return ls, ls - z2


def _sb_prompt_kernel(q_ref, kv_ref, gn_ref, o_ref):
    i = pl.program_id(1)
    tk = SB_TK
    nr = H_A * TQ
    r = lax.broadcasted_iota(jnp.int32, (tk, tk), 0)
    c = lax.broadcasted_iota(jnp.int32, (tk, tk), 1)
    later01 = (r > c).astype(BF16)
    t_row = i * TQ + lax.broadcasted_iota(jnp.int32, (nr, 1), 0) % TQ
    s_off = lax.broadcasted_iota(jnp.int32, (1, tk), 1)
    qs = [q_ref[:, h * DH_A:(h + 1) * DH_A] for h in range(H_A)]

    def terms(j, diag):
        start = pl.multiple_of(j * tk, tk)
        z2 = jnp.concatenate([_mm(qs[h], kv_ref[pl.ds(start, tk), h * DH_A:(h + 1) * DH_A], _NT)
                              for h in range(H_A)], axis=0)
        ls, lk = _sb_log2_terms(z2)
        before = (start + s_off) < t_row if diag else None
        if diag:
            lk = jnp.where(before, lk, 0.0)
        return ls + _mm(lk.astype(BF16), later01), jnp.sum(lk, axis=-1, keepdims=True), before

    def weighted_values(j, logw, before):
        start = pl.multiple_of(j * tk, tk)
        w = jnp.exp2(logw)
        if before is not None:
            w = jnp.where(before, w, 0.0)
        wh = w.astype(BF16)
        return jnp.concatenate([_mm(wh[h * TQ:(h + 1) * TQ],
                                    kv_ref[pl.ds(start, tk), W_A + h * DH_A:W_A + (h + 1) * DH_A])
                                for h in range(H_A)], axis=0)

    def step(j, carry, diag):
        run, acc = carry
        lw, tot, before = terms(j, diag)
        return run + tot, acc + weighted_values(j, lw + run, before)

    def two_steps(j, carry):
        run, acc = carry
        lw_a, tot_a, _ = terms(j, False)
        lw_b, tot_b, _ = terms(j - 1, False)
        run_b = run + tot_a
        acc = acc + weighted_values(j, lw_a + run, None) + weighted_values(j - 1, lw_b + run_b, None)
        return run_b + tot_b, acc

    jd = i // (tk // TQ)
    carry = (jnp.zeros((nr, 1), F32), jnp.zeros((nr, DH_A), F32))
    carry = step(jd, carry, True)
    carry = lax.fori_loop(0, jd // 2, lambda p, cr: two_steps(jd - 1 - 2 * p, cr), carry)
    carry = lax.cond(jd % 2 == 1, lambda cr: step(0, cr, False), lambda cr: cr, carry)
    o = carry[1]
    o = o * lax.rsqrt(jnp.mean(o * o, axis=-1, keepdims=True) + EPS)
    o = jnp.concatenate([o[h * TQ:(h + 1) * TQ] for h in range(H_A)], axis=1)
    o_ref[...] = (o * gn_ref[...]).astype(BF16)


def _sb_prompt(aq, akv, gn, batch, seq):
    nq = seq // TQ
    return pl.pallas_call(
        _sb_prompt_kernel,
        grid=(batch, nq),
        in_specs=[pl.BlockSpec((TQ, W_A), lambda b, i: (b * nq + i, 0)),
                  pl.BlockSpec((seq, 2 * W_A), lambda b, i: (b, 0)),
                  pl.BlockSpec((1, W_A), lambda b, i: (0, 0))],
        out_specs=pl.BlockSpec((TQ, W_A), lambda b, i: (b * nq + i, 0)),
        out_shape=jax.ShapeDtypeStruct((batch * seq, W_A), BF16),
        compiler_params=_cparams(("parallel", "arbitrary")),
        name="sb_prompt",
    )(aq, akv, gn)


SB_DEC_GROUP = 2
NSA_DEC_GROUP = 4


def _page_copies(pt_ref, cache_ref, buf_ref, sem_ref, layer, step, slot, n_pages, page):
    group = buf_ref.shape[1]
    return [pltpu.make_async_copy(cache_ref.at[pt_ref[step * group + g, p], layer],
                                  buf_ref.at[slot, g, :, pl.ds(p * page, page)], sem_ref.at[slot])
            for g in range(group) for p in range(n_pages)]


def _interleave(stage_generators):
    live = list(stage_generators)
    while live:
        live = [gen for gen in live if next(gen, _DONE) is not _DONE]


_DONE = object()


def _paged_prefetch(pt_ref, cache_ref, buf_ref, sem_ref, layer, n_pages, page):
    s = pl.program_id(0)
    slot = s % 2

    @pl.when(s == 0)
    def _():
        for cp in _page_copies(pt_ref, cache_ref, buf_ref, sem_ref, layer, 0, 0, n_pages, page):
            cp.start()

    @pl.when(s + 1 < pl.num_programs(0))
    def _():
        for cp in _page_copies(pt_ref, cache_ref, buf_ref, sem_ref, layer, s + 1, 1 - slot, n_pages, page):
            cp.start()

    for cp in _page_copies(pt_ref, cache_ref, buf_ref, sem_ref, layer, s, slot, n_pages, page):
        cp.wait()
    return slot


def _chunk_rows(x, n_chunks):
    return jnp.concatenate([x[:, c * LANES:(c + 1) * LANES] for c in range(n_chunks)], axis=0)


def _chunk_lanes(x, n_chunks):
    return jnp.concatenate([x[c * SUBLANES:(c + 1) * SUBLANES] for c in range(n_chunks)], axis=1)


def _sb_decode_kernel(pt_ref, q_ref, gn_ref, cache_ref, o_ref, buf_ref, sem_ref, *, layer, n_pages, page):
    slot = _paged_prefetch(pt_ref, cache_ref, buf_ref, sem_ref, layer, n_pages, page)
    group = buf_ref.shape[1]
    hr = SUBLANES
    nc = n_pages * page // LANES
    rr = lax.broadcasted_iota(jnp.int32, (hr, W_A), 0)
    cc = lax.broadcasted_iota(jnp.int32, (hr, W_A), 1)
    own = rr == cc // DH_A
    r = lax.broadcasted_iota(jnp.int32, (LANES, LANES), 0)
    c = lax.broadcasted_iota(jnp.int32, (LANES, LANES), 1)
    later01 = (r > c).astype(BF16)
    r = lax.broadcasted_iota(jnp.int32, (hr * nc, hr * nc), 0)
    c = lax.broadcasted_iota(jnp.int32, (hr * nc, hr * nc), 1)
    later_chunk01 = ((r % hr == c % hr) & (c // hr > r // hr)).astype(BF16)
    group1 = _group_ones(W_A, DH_A)

    def one_sequence(g):
        s = pl.program_id(0) * group + g
        q = q_ref[pl.ds(s, 1), :]
        q8 = jnp.where(own, q, 0.0).astype(BF16)
        z2 = _mm(q8, buf_ref[slot, g, 0:W_A, :].astype(BF16))
        yield
        ls, lk = _sb_log2_terms(z2)
        ls_c, lk_c = _chunk_rows(ls, nc), _chunk_rows(lk, nc)
        inner = _mm_x01(lk_c, later01)
        tot = jnp.broadcast_to(jnp.sum(lk_c, axis=-1, keepdims=True), (hr * nc, LANES))
        yield
        outer = _mm_01x(later_chunk01, tot)
        w = _chunk_lanes(jnp.exp2(ls_c + inner + outer), nc)
        yield
        o8 = _mm(w.astype(BF16), buf_ref[slot, g, W_A:2 * W_A, :].astype(BF16), _NT)
        yield
        o = jnp.sum(jnp.where(own, o8, 0.0), axis=0, keepdims=True)
        ms = _mm_x01(o * o, group1) * (1.0 / DH_A)
        o_ref[pl.ds(s, 1), :] = o * lax.rsqrt(ms + EPS) * gn_ref[...]

    _interleave([one_sequence(g) for g in range(group)])


def _sb_decode(page_table, aq, gn, cache, layer):
    n_seq, n_pages = page_table.shape
    page = cache.shape[3]
    kern = functools.partial(_sb_decode_kernel, layer=layer, n_pages=n_pages, page=page)
    return pl.pallas_call(
        kern,
        grid_spec=pltpu.PrefetchScalarGridSpec(
            num_scalar_prefetch=1,
            grid=(n_seq // SB_DEC_GROUP,),
            in_specs=[pl.BlockSpec((n_seq, W_A), lambda s, pt: (0, 0)),
                      pl.BlockSpec((1, W_A), lambda s, pt: (0, 0)),
                      pl.BlockSpec(memory_space=pl.ANY)],
            out_specs=pl.BlockSpec((n_seq, W_A), lambda s, pt: (0, 0)),
            scratch_shapes=[pltpu.VMEM((2, SB_DEC_GROUP, 2 * W_A, n_pages * page), F32),
                            pltpu.SemaphoreType.DMA((2,))]),
        out_shape=jax.ShapeDtypeStruct((n_seq, W_A), F32),
        compiler_params=_cparams(("arbitrary",)),
        name="sb_decode",
    )(page_table, aq, gn, cache)


def _unit_lower_inverse(a_mats, n):
    r = lax.broadcasted_iota(jnp.int32, (n, n), 0)
    c = lax.broadcasted_iota(jnp.int32, (n, n), 1)
    eye = (r == c).astype(F32)
    base = 16
    mm1 = lambda a, b: _mm(a.astype(BF16), b.astype(BF16))
    xs = [jnp.where(r // base == c // base, -a, 0.0) for a in a_mats]
    invs = [eye + x for x in xs]
    span = 1
    while 2 * span < base:
        xs = [mm1(x, x) for x in xs]
        invs = [inv + mm1(inv, x) for inv, x in zip(invs, xs)]
        span *= 2
    size = base
    while size < DN_CHUNK:
        off_mask = (r // (2 * size) == c // (2 * size)) & (r // size != c // size)
        mids = [mm1(jnp.where(off_mask, a, 0.0), inv) for a, inv in zip(a_mats, invs)]
        invs = [inv - mm1(inv, mid) for inv, mid in zip(invs, mids)]
        size *= 2
    return invs


def _l2n(x):
    return x * lax.rsqrt(jnp.sum(x * x, axis=-1, keepdims=True) + EPS)


def _dn_prompt_kernel(raw_ref, z_ref, small_ref, cw_ref, par_ref, gn_ref, o_ref, s_ref, xbuf, s_scr):
    ci = pl.program_id(1)
    tb = DN_BLOCK
    nk = H_B * DK_B

    @pl.when(ci == 0)
    def _():
        xbuf[0:SUBLANES, :] = jnp.zeros((SUBLANES, CONV_CH), F32)
        s_scr[...] = jnp.zeros_like(s_scr)

    xbuf[SUBLANES:SUBLANES + tb, :] = raw_ref[...]
    conv = xbuf[SUBLANES - 3:SUBLANES - 3 + tb, :] * cw_ref[0:1, :]
    for i in range(1, CONV_W):
        conv = conv + xbuf[SUBLANES - 3 + i:SUBLANES - 3 + i + tb, :] * cw_ref[i:i + 1, :]
    xbuf[0:SUBLANES, :] = xbuf[tb:tb + SUBLANES, :]
    qkv = _silu(conv)

    small = small_ref[...]
    beta_all = _sigmoid(small)
    g_all = -jnp.exp(par_ref[0:1, :]) * _softplus(small + par_ref[1:2, :])

    r = lax.broadcasted_iota(jnp.int32, (tb, tb), 0)
    c = lax.broadcasted_iota(jnp.int32, (tb, tb), 1)
    same = (r // DN_CHUNK) == (c // DN_CHUNK)
    lower = same & (r >= c)
    strict = same & (r > c)
    lower01 = lower.astype(BF16)
    upper01 = (same & (r <= c)).astype(BF16)
    same01 = same.astype(BF16)
    ones8 = jnp.ones((SUBLANES, tb), BF16)

    g_cum = _mm_01x(lower01, g_all, parts=3)
    g_tot = _mm_01x(same01, g_all, parts=3)
    e_cum = jnp.exp(g_cum)
    e_rest = jnp.exp(g_tot - g_cum)
    e_tot = jnp.exp(g_tot)

    heads = range(H_B)
    lanes = [SMALL_DECAY + h for h in heads]
    qs = [_l2n(qkv[:, h * DK_B:(h + 1) * DK_B]) * (DK_B ** -0.5) for h in heads]
    ks = [_l2n(qkv[:, nk + h * DK_B:nk + (h + 1) * DK_B]) for h in heads]
    vs = [qkv[:, 2 * nk + h * DV_B:2 * nk + (h + 1) * DV_B] for h in heads]
    betas = [beta_all[:, SMALL_BETA + h:SMALL_BETA + h + 1] for h in heads]
    upper_f = upper01.astype(F32)
    gc_rows = [_mm_01x(ones8, g_all[:, ln:ln + 1] * upper_f, parts=3)[0:1, :] for ln in lanes]
    decays = [jnp.exp(jnp.where(lower, g_cum[:, ln:ln + 1] - gr, -jnp.inf)) for ln, gr in zip(lanes, gc_rows)]
    kbs = [k * b for k, b in zip(ks, betas)]
    khs = [k.astype(BF16) for k in ks]
    qhs = [q.astype(BF16) for q in qs]
    a_mats = [jnp.where(strict, _mm(kb.astype(BF16), kh, _NT) * d, 0.0) for kb, kh, d in zip(kbs, khs, decays)]
    invs = _unit_lower_inverse(a_mats, tb)
    sols = [_mm3(inv, jnp.concatenate([v * b, kb * e_cum[:, ln:ln + 1]], axis=1))
            for inv, v, b, kb, ln in zip(invs, vs, betas, kbs, lanes)]
    u_bases = [sol[:, 0:DV_B] for sol in sols]
    whs = [sol[:, DV_B:].astype(BF16) for sol in sols]
    qkhs = [(_mm(qh, kh, _NT) * d).astype(BF16) for qh, kh, d in zip(qhs, khs, decays)]
    q_decs = [(q * e_cum[:, ln:ln + 1]).astype(BF16) for q, ln in zip(qs, lanes)]
    k_decs = [(k * e_rest[:, ln:ln + 1]).astype(BF16) for k, ln in zip(ks, lanes)]
    states = [s_scr[h] for h in heads]
    outs = [[] for _ in heads]
    for n in range(tb // DN_CHUNK):
        lo, hi = n * DN_CHUNK, (n + 1) * DN_CHUNK
        shs = [st.astype(BF16) for st in states]
        us = [ub[lo:hi] - _mm(wh[lo:hi], sh) for ub, wh, sh in zip(u_bases, whs, shs)]
        uhs = [u.astype(BF16) for u in us]
        for h in heads:
            outs[h].append(_mm(q_decs[h][lo:hi], shs[h]) + _mm(qkhs[h][lo:hi, lo:hi], uhs[h]))
        states = [st * e_tot[lo:lo + 1, ln:ln + 1] + _mm(kd[lo:hi], uh, _TN)
                  for st, ln, kd, uh in zip(states, lanes, k_decs, uhs)]
    for h in heads:
        s_scr[h] = states[h]
        o = jnp.concatenate(outs[h], axis=0)
        o = o * lax.rsqrt(jnp.mean(o * o, axis=-1, keepdims=True) + EPS) * gn_ref[...]
        o_ref[:, h * DV_B:(h + 1) * DV_B] = (o * _silu(z_ref[:, h * DV_B:(h + 1) * DV_B])).astype(BF16)

    @pl.when(ci == pl.num_programs(1) - 1)
    def _():
        s_ref[...] = s_scr[...]


def _dn_prompt(braw, bz, small, cw, par, gn, batch, seq):
    tb = DN_BLOCK
    nb = seq // tb
    row = lambda b, i: (b * nb + i, 0)
    const = lambda b, i: (0, 0)
    return pl.pallas_call(
        _dn_prompt_kernel,
        grid=(batch, nb),
        in_specs=[pl.BlockSpec((tb, CONV_CH), row), pl.BlockSpec((tb, W_B), row), pl.BlockSpec((tb, LANES), row),
                  pl.BlockSpec((CONV_W, CONV_CH), const), pl.BlockSpec((2, LANES), const),
                  pl.BlockSpec((1, DV_B), const)],
        out_specs=[pl.BlockSpec((tb, W_B), row),
                   pl.BlockSpec((None, H_B, DK_B, DV_B), lambda b, i: (b, 0, 0, 0))],
        out_shape=[jax.ShapeDtypeStruct((batch * seq, W_B), BF16),
                   jax.ShapeDtypeStruct((batch, H_B, DK_B, DV_B), F32)],
        scratch_shapes=[pltpu.VMEM((SUBLANES + tb, CONV_CH), F32), pltpu.VMEM((H_B, DK_B, DV_B), F32)],
        compiler_params=_cparams(("parallel", "arbitrary")),
        name="dn_prompt",
    )(braw, bz, small, cw, par, gn)


DN_DEC_SEQS = 8


def _dn_decode_kernel(raw_ref, z_ref, small_ref, conv_ref, st_ref, cw_ref, par_ref, gn_ref, o_ref, s_ref):
    nk = H_B * DK_B
    conv = raw_ref[...] * cw_ref[CONV_W - 1:CONV_W, :]
    for i in range(CONV_W - 1):
        conv = conv + conv_ref[:, i, :] * cw_ref[i:i + 1, :]
    qkv = _silu(conv)
    small = small_ref[...]
    beta_all = _sigmoid(small)
    eg_all = jnp.exp(-jnp.exp(par_ref[0:1, :]) * _softplus(small + par_ref[1:2, :]))
    zero7 = jnp.zeros((SUBLANES - 1, DK_B), F32)
    outs = [[None] * DN_DEC_SEQS for _ in range(H_B)]

    def one_state(h, s, q, k, v, beta, eg, qk):
        state = st_ref[s, h]
        kq = jnp.concatenate([k[s:s + 1], q[s:s + 1], zero7[0:SUBLANES - 2]], axis=0).astype(BF16)
        proj = _mm(kq, state.astype(BF16))
        yield
        e = eg[s:s + 1]
        u = beta[s:s + 1] * (v[s:s + 1] - e * proj[0:1])
        outs[h][s] = e * proj[1:2] + qk[s:s + 1] * u
        k8 = jnp.concatenate([k[s:s + 1], zero7], axis=0).astype(BF16)
        u8 = jnp.concatenate([u, zero7], axis=0).astype(BF16)
        s_ref[s, h] = state * e + _mm(k8, u8, _TN)

    chains = []
    for h in range(H_B):
        q = _l2n(qkv[:, h * DK_B:(h + 1) * DK_B]) * (DK_B ** -0.5)
        k = _l2n(qkv[:, nk + h * DK_B:nk + (h + 1) * DK_B])
        v = qkv[:, 2 * nk + h * DV_B:2 * nk + (h + 1) * DV_B]
        beta = beta_all[:, SMALL_BETA + h:SMALL_BETA + h + 1]
        eg = eg_all[:, SMALL_DECAY + h:SMALL_DECAY + h + 1]
        qk = jnp.sum(q * k, axis=-1, keepdims=True)
        chains += [one_state(h, s, q, k, v, beta, eg, qk) for s in range(DN_DEC_SEQS)]
    _interleave(chains)
    for h in range(H_B):
        o = jnp.concatenate(outs[h], axis=0)
        o = o * lax.rsqrt(jnp.mean(o * o, axis=-1, keepdims=True) + EPS) * gn_ref[...]
        o_ref[:, h * DV_B:(h + 1) * DV_B] = (o * _silu(z_ref[:, h * DV_B:(h + 1) * DV_B])).astype(BF16)


def _dn_decode(braw, bz, small, state_conv, state_delta, cw, par, gn, layer):
    n_seq = braw.shape[0]
    ts = DN_DEC_SEQS
    row = lambda i: (i, 0)
    const = lambda i: (0, 0)
    return pl.pallas_call(
        _dn_decode_kernel,
        grid=(n_seq // ts,),
        in_specs=[pl.BlockSpec((ts, CONV_CH), row), pl.BlockSpec((ts, W_B), row), pl.BlockSpec((ts, LANES), row),
                  pl.BlockSpec((ts, None, CONV_W - 1, CONV_CH), lambda i: (i, layer, 0, 0)),
                  pl.BlockSpec((ts, None, H_B, DK_B, DV_B), lambda i: (i, layer, 0, 0, 0)),
                  pl.BlockSpec((CONV_W, CONV_CH), const), pl.BlockSpec((2, LANES), const),
                  pl.BlockSpec((1, DV_B), const)],
        out_specs=[pl.BlockSpec((ts, W_B), row),
                   pl.BlockSpec((ts, H_B, DK_B, DV_B), lambda i: (i, 0, 0, 0))],
        out_shape=[jax.ShapeDtypeStruct((n_seq, W_B), BF16),
                   jax.ShapeDtypeStruct((n_seq, H_B, DK_B, DV_B), F32)],
        compiler_params=_cparams(("parallel",)),
        name="dn_decode",
    )(braw, bz, small, state_conv, state_delta, cw, par, gn)


def _block_summaries(rows, kn0):
    n = rows.shape[0]
    m = jnp.mean(rows.reshape(n // CMP_BLOCK, CMP_BLOCK, LANES), axis=1)
    lane = lax.broadcasted_iota(jnp.int32, m.shape, 1)
    is_k = lane < DH_C
    ms = jnp.sum(jnp.where(is_k, m * m, 0.0), axis=-1, keepdims=True) * (1.0 / DH_C)
    return jnp.where(is_k, m * lax.rsqrt(ms + EPS) * kn0, m)


def _nsa_cmp_kernel(rows_ref, kn0_ref, o_ref):
    o_ref[...] = _block_summaries(rows_ref[...], kn0_ref[...]).astype(BF16)


def _nsa_cmp(nsa_rows, kn0, batch, seq):
    nb = seq // CMP_BLOCK
    return pl.pallas_call(
        _nsa_cmp_kernel,
        grid=(batch,),
        in_specs=[pl.BlockSpec((seq, LANES), lambda b: (b, 0)), pl.BlockSpec((1, LANES), lambda b: (0, 0))],
        out_specs=pl.BlockSpec((nb, LANES), lambda b: (b, 0)),
        out_shape=jax.ShapeDtypeStruct((batch * nb, LANES), BF16),
        compiler_params=_cparams(("parallel",)),
        name="nsa_cmp",
    )(nsa_rows, kn0)


def _alibi_slopes2(shape, axis, per_head):
    h = lax.broadcasted_iota(jnp.int32, shape, axis) // per_head
    return jnp.exp2((h + 1).astype(F32) * (-8.0 / H_C)) * LOG2E


def _masked_softmax2(s2, mask, axis):
    s2 = jnp.where(mask, s2, NEG_INF)
    m = jnp.max(s2, axis=axis, keepdims=True)
    e = jnp.where(mask, jnp.exp2(s2 - m), 0.0)
    return e / jnp.maximum(jnp.sum(e, axis=axis, keepdims=True), 1e-30)


def _nsa_prompt_kernel(qt_ref, small_ref, cmp_ref, kv_ref, kvt_ref, gn_ref, o_ref, s_scr, *, n_blocks):
    i = pl.program_id(1)
    nr = H_C * TQ
    tk = NSA_TK
    qt = jnp.concatenate([qt_ref[h * DH_C:(h + 1) * DH_C, :] for h in range(H_C)], axis=1)
    qt_pad = jnp.concatenate([qt, jnp.zeros_like(qt)], axis=0)
    t_col = i * TQ + lax.broadcasted_iota(jnp.int32, (1, nr), 1) % TQ
    t_one = t_col[:, 0:TQ]
    slope_c = _alibi_slopes2((1, nr), 1, TQ)

    kcb = cmp_ref[:, 0:DH_C]
    vcb = cmp_ref[:, DH_C:2 * DH_C]
    blk = lax.broadcasted_iota(jnp.int32, (n_blocks, 1), 0)
    blk_mid = (blk * CMP_BLOCK).astype(F32) + 0.5 * (CMP_BLOCK - 1)
    vis_c = ((blk + 1) * CMP_BLOCK - 1) <= t_col
    s_c = _mm(kcb, qt) - slope_c * (t_col.astype(F32) - blk_mid)
    p_c = _masked_softmax2(s_c, vis_c, 0)
    o_c = _mm(vcb, p_c.astype(BF16), _TN)

    imp = p_c[:, 0:TQ]
    for h in range(1, H_C):
        imp = imp + p_c[:, h * TQ:(h + 1) * TQ]
    cur = t_one // SEL_BLOCK
    jb = lax.broadcasted_iota(jnp.int32, (n_blocks, TQ), 0)
    forced = (jb == 0) | (jb == cur) | (jb == cur - 1)
    score = jnp.where(jb > cur, -1.0, jnp.where(forced, 2.0 * H_C, imp))
    rank = jnp.zeros((n_blocks, TQ), F32)
    for n in range(n_blocks):
        other = score[n:n + 1, :]
        rank = rank + ((other > score) | ((other == score) & (jb > n))).astype(F32)
    sel = ((rank < float(min(TOP_K_BLOCKS, n_blocks))) & (jb <= cur)).astype(BF16)

    bpt = tk // SEL_BLOCK
    er = lax.broadcasted_iota(jnp.int32, (tk, n_blocks), 0) // SEL_BLOCK
    ec = lax.broadcasted_iota(jnp.int32, (tk, n_blocks), 1)
    s_off = lax.broadcasted_iota(jnp.int32, (tk, 1), 0)
    row = lax.broadcasted_iota(jnp.int32, (LANES, tk), 0)

    def attend(j_lo, j_hi, kv_lo, selected):
        def score_tile(j):
            start = pl.multiple_of(j * tk, tk)
            kv = kv_ref[pl.ds(start, tk), kv_lo:kv_lo + LANES]
            s_pos = start + s_off
            valid = s_pos <= t_one
            if selected:
                valid = valid & (_mm((er + j * bpt == ec).astype(BF16), sel) > 0.5)
            else:
                valid = valid & (t_one - s_pos < WINDOW)
            dist = jnp.where(valid, (s_pos - t_one).astype(F32), NEG_INF)
            s2 = _mm(kv, qt_pad) + slope_c * jnp.concatenate([dist] * H_C, axis=1)
            s_scr[j] = s2
            return jnp.max(s2.reshape(tk // SUBLANES, SUBLANES, nr), axis=0)

        p_lo, p_hi = j_lo // 2, (j_hi + 1) // 2

        def scores(p, mx):
            return jnp.maximum(mx, jnp.maximum(score_tile(2 * p), score_tile(2 * p + 1)))

        mx = lax.fori_loop(p_lo, p_hi, scores, jnp.full((SUBLANES, nr), NEG_INF, F32))
        m = jnp.max(mx, axis=0, keepdims=True)

        def weigh_tile(j):
            start = pl.multiple_of(j * tk, tk)
            kvt = kvt_ref[kv_lo:kv_lo + LANES, pl.ds(start, tk)]
            ones_vt = jnp.where(row < DH_C, 1.0, kvt)
            p = jnp.exp2(s_scr[j] - m)
            return _mm(ones_vt, p.astype(BF16))

        def weigh(p, acc):
            return acc + (weigh_tile(2 * p) + weigh_tile(2 * p + 1))

        acc = lax.fori_loop(p_lo, p_hi, weigh, jnp.zeros((LANES, nr), F32))
        return acc[DH_C:2 * DH_C] / acc[0:DH_C]

    j_end = (i * TQ) // tk + 1
    o_s = attend(0, j_end, 0, True)
    j_win = jnp.maximum(i * TQ - (WINDOW - 1), 0) // tk
    o_w = attend(j_win, j_end, 2 * DH_C, False)

    small_t = small_ref[...].T

    def gate_row(rr):
        g = jnp.concatenate([small_t[SMALL_GATE + 3 * h + rr:SMALL_GATE + 3 * h + rr + 1, :]
                             for h in range(H_C)], axis=1)
        return _sigmoid(g)

    o = gate_row(0) * o_c + gate_row(1) * o_s + gate_row(2) * o_w
    o = o * lax.rsqrt(jnp.mean(o * o, axis=0, keepdims=True) + EPS)
    gain = jnp.concatenate([jnp.broadcast_to(gn_ref[:, h:h + 1], (DH_C, TQ)) for h in range(H_C)], axis=1)
    o = o * gain
    pairs = [jnp.concatenate([o[:, (2 * g) * TQ:(2 * g + 1) * TQ], o[:, (2 * g + 1) * TQ:(2 * g + 2) * TQ]], axis=0).T
             for g in range(H_C // 2)]
    o_ref[...] = jnp.concatenate(pairs, axis=1).astype(BF16)


def _nsa_prompt(cqt, small, cmp, ckv, ckvt, gn_cols, batch, seq):
    assert seq % (2 * NSA_TK) == 0, "key tiles are processed in pairs"
    nq = seq // TQ
    nb = seq // CMP_BLOCK
    row = lambda b, i: (b * nq + i, 0)
    kern = functools.partial(_nsa_prompt_kernel, n_blocks=nb)
    return pl.pallas_call(
        kern,
        grid=(batch, nq),
        in_specs=[pl.BlockSpec((W_C, TQ), lambda b, i: (0, b * nq + i)), pl.BlockSpec((TQ, LANES), row),
                  pl.BlockSpec((nb, LANES), lambda b, i: (b, 0)),
                  pl.BlockSpec((seq, 4 * DH_C), lambda b, i: (b, 0)),
                  pl.BlockSpec((4 * DH_C, seq), lambda b, i: (0, b)),
                  pl.BlockSpec((DH_C, H_C), lambda b, i: (0, 0))],
        out_specs=pl.BlockSpec((TQ, W_C), row),
        out_shape=jax.ShapeDtypeStruct((batch * seq, W_C), BF16),
        scratch_shapes=[pltpu.VMEM((seq // NSA_TK, NSA_TK, H_C * TQ), F32)],
        compiler_params=_cparams(("parallel", "arbitrary")),
        name="nsa_prompt",
    )(cqt, small, cmp, ckv, ckvt, gn_cols)


def _nsa_decode_kernel(pt_ref, q_ref, small_ref, cur_nsa_ref, cur_win_ref, win_ref, kn0_ref, gn_ref, member_ref,
                       cache_ref, o_ref, buf_ref, sem_ref, *, layer, n_pages, page):
    slot = _paged_prefetch(pt_ref, cache_ref, buf_ref, sem_ref, layer, n_pages, page)
    group = buf_ref.shape[1]
    _interleave([_nsa_decode_one(pl.program_id(0) * group + g, buf_ref.at[slot, g], win_ref.at[g], q_ref,
                                 small_ref, cur_nsa_ref, cur_win_ref, kn0_ref, gn_ref, member_ref, o_ref,
                                 n_pages * page) for g in range(group)])


def _nsa_decode_one(s, past_ref, win_ref, q_ref, small_ref, cur_nsa_ref, cur_win_ref, kn0_ref, gn_ref, member_ref,
                    o_ref, n_past):
    n_cmp = n_past // CMP_BLOCK
    cur = n_past // SEL_BLOCK
    t_f = float(n_past)
    hr = SUBLANES

    qrow = q_ref[pl.ds(s, 1), :]
    q8 = jnp.concatenate([qrow[:, h * DH_C:(h + 1) * DH_C] for h in range(H_C)]
                         + [jnp.zeros((hr - H_C, DH_C), F32)], axis=0).astype(BF16)
    slope = _alibi_slopes2((hr, 1), 0, 1)

    summ = _mm(past_ref[0:2 * DH_C, :].astype(BF16), member_ref[...], _NT)[:, 0:n_cmp] * (1.0 / CMP_BLOCK)
    yield
    km = summ[0:DH_C]
    ms = jnp.sum(km * km, axis=0, keepdims=True) * (1.0 / DH_C)
    kcb = (km * lax.rsqrt(ms + EPS) * kn0_ref[...]).astype(BF16)
    vcb = summ[DH_C:2 * DH_C].astype(BF16)
    blk = lax.broadcasted_iota(jnp.int32, (1, n_cmp), 1)
    blk_mid = (blk * CMP_BLOCK).astype(F32) + 0.5 * (CMP_BLOCK - 1)
    s_c = _mm(q8, kcb) - slope * (t_f - blk_mid)
    yield
    p_c = _masked_softmax2(s_c, blk >= 0, -1)
    o_c = _mm(p_c.astype(BF16), vcb, _NT)

    hrow = lax.broadcasted_iota(jnp.int32, (hr, n_cmp), 0)
    imp = jnp.sum(jnp.where(hrow < H_C, p_c, 0.0), axis=0, keepdims=True)
    imp = jnp.concatenate([imp, jnp.zeros((1, LANES - n_cmp), F32)], axis=1)
    jb = lax.broadcasted_iota(jnp.int32, (1, LANES), 1)
    forced = (jb == 0) | (jb == cur) | (jb == cur - 1)
    score = jnp.where(jb > cur, -1.0, jnp.where(forced, 2.0 * H_C, imp))
    srow = jnp.broadcast_to(score, (LANES, LANES))
    scol = srow.T
    ri = lax.broadcasted_iota(jnp.int32, (LANES, LANES), 0)
    ci = lax.broadcasted_iota(jnp.int32, (LANES, LANES), 1)
    ahead = (scol > srow) | ((scol == srow) & (ri < ci))
    rank = jnp.sum(ahead.astype(F32), axis=0, keepdims=True)
    sel = ((rank < float(min(TOP_K_BLOCKS, cur + 1))) & (jb <= cur)).astype(BF16)
    yield
    sel8 = jnp.broadcast_to(sel, (hr, LANES))
    picked = _mm(sel8, member_ref[...]) > 0.5

    ks = past_ref[2 * DH_C:3 * DH_C, :].astype(BF16)
    vs = past_ref[3 * DH_C:4 * DH_C, :].astype(BF16)
    pos = lax.broadcasted_iota(jnp.int32, (1, n_past), 1).astype(F32)
    s_s = _mm(q8, ks) - slope * (t_f - pos)
    cur_row = cur_nsa_ref[pl.ds(s, 1), :]
    q8f = q8.astype(F32)
    s_cur = jnp.sum(q8f * cur_row[:, 2 * DH_C:3 * DH_C].astype(BF16).astype(F32), axis=-1, keepdims=True)
    yield
    s_s = jnp.where(picked, s_s, NEG_INF)
    m = jnp.maximum(jnp.max(s_s, axis=-1, keepdims=True), s_cur)
    e = jnp.where(picked, jnp.exp2(s_s - m), 0.0)
    e_cur = jnp.exp2(s_cur - m)
    den = jnp.sum(e, axis=-1, keepdims=True) + e_cur
    o_s = (_mm(e.astype(BF16), vs, _NT) + e_cur * cur_row[:, 3 * DH_C:4 * DH_C]) / den

    yield
    n_buf = win_ref.shape[1]
    kw = win_ref[0:DH_C, :].astype(BF16)
    vw = win_ref[DH_C:2 * DH_C, :].astype(BF16)
    wpos = n_past - n_buf + lax.broadcasted_iota(jnp.int32, (1, n_buf), 1)
    vis_w = (n_past - wpos < WINDOW) & (wpos >= 0)
    s_w = _mm(q8, kw) - slope * (t_f - wpos.astype(F32))
    cur_w = cur_win_ref[pl.ds(s, 1), :]
    sw_cur = jnp.sum(q8f * cur_w[:, 0:DH_C].astype(BF16).astype(F32), axis=-1, keepdims=True)
    yield
    s_w = jnp.where(vis_w, s_w, NEG_INF)
    m = jnp.maximum(jnp.max(s_w, axis=-1, keepdims=True), sw_cur)
    e = jnp.where(vis_w, jnp.exp2(s_w - m), 0.0)
    e_cur = jnp.exp2(sw_cur - m)
    den = jnp.sum(e, axis=-1, keepdims=True) + e_cur
    o_w = (_mm(e.astype(BF16), vw, _NT) + e_cur * cur_w[:, DH_C:2 * DH_C]) / den

    yield
    gates = jnp.broadcast_to(_sigmoid(small_ref[pl.ds(s, 1), :]), (hr, LANES))
    gl = lax.broadcasted_iota(jnp.int32, (hr, LANES), 1)
    gh = lax.broadcasted_iota(jnp.int32, (hr, LANES), 0)

    def gate_col(rr):
        return jnp.sum(jnp.where(gl == SMALL_GATE + 3 * gh + rr, gates, 0.0), axis=-1, keepdims=True)

    o = gate_col(0) * o_c + gate_col(1) * o_s + gate_col(2) * o_w
    o = o * lax.rsqrt(jnp.mean(o * o, axis=-1, keepdims=True) + EPS)
    orow = jnp.concatenate([o[h:h + 1] for h in range(H_C)], axis=1)
    o_ref[pl.ds(s, 1), :] = orow * gn_ref[...]


def _nsa_decode(page_table, cq, small, cur_nsa, cur_win, state_win, kn0, gn, cache, layer):
    n_seq, n_pages = page_table.shape
    page = cache.shape[3]
    n_buf = state_win.shape[3]
    n_past = n_pages * page
    assert CMP_BLOCK == SEL_BLOCK and n_past // SEL_BLOCK < LANES
    member = (lax.broadcasted_iota(jnp.int32, (LANES, n_past), 0)
              == lax.broadcasted_iota(jnp.int32, (LANES, n_past), 1) // SEL_BLOCK).astype(BF16)
    kern = functools.partial(_nsa_decode_kernel, layer=layer, n_pages=n_pages, page=page)
    whole = lambda shape: pl.BlockSpec(shape, lambda s, pt: (0,) * len(shape))
    return pl.pallas_call(
        kern,
        grid_spec=pltpu.PrefetchScalarGridSpec(
            num_scalar_prefetch=1,
            grid=(n_seq // NSA_DEC_GROUP,),
            in_specs=[whole((n_seq, W_C)), whole((n_seq, LANES)), whole((n_seq, 4 * DH_C)), whole((n_seq, 2 * DH_C)),
                      pl.BlockSpec((NSA_DEC_GROUP, None, 2 * DH_C, n_buf), lambda s, pt: (s, layer, 0, 0)),
                      whole((DH_C, 1)), whole((1, W_C)), whole((LANES, n_past)),
                      pl.BlockSpec(memory_space=pl.ANY)],
            out_specs=whole((n_seq, W_C)),
            scratch_shapes=[pltpu.VMEM((2, NSA_DEC_GROUP, 4 * DH_C, n_pages * page), F32),
                            pltpu.SemaphoreType.DMA((2,))]),
        out_shape=jax.ShapeDtypeStruct((n_seq, W_C), F32),
        compiler_params=_cparams(("arbitrary",)),
        name="nsa_decode",
    )(page_table, cq, small, cur_nsa, cur_win, state_win, kn0, gn, member, cache)


def _pack_w_in(w_in):
    wt = jnp.transpose(w_in, (2, 0, 1))
    widths = [W_A, W_A, W_A, H_B * DK_B, H_B * DK_B, W_B, W_B, H_B, H_B, W_C] + [DH_C] * 6 + [3 * H_C]
    rows, start = [], 0
    for w in widths:
        rows.append(wt[start:start + w])
        start += w
    (a_q, a_k, a_v, b_q, b_k, b_v, b_z, b_beta, b_decay, c_q, c_kc, c_vc, c_ks, c_vs, c_kw, c_vw, c_gate) = rows
    pad = jnp.zeros((LANES - 2 * H_B - 3 * H_C,) + wt.shape[1:], wt.dtype)
    packed = jnp.concatenate([a_k, a_v, a_q, b_q, b_k, b_v, b_z, c_q, c_kc, c_vc, c_ks, c_vs, c_kw, c_vw,
                              b_beta, b_decay, c_gate, pad], axis=0)
    return jnp.transpose(packed.astype(BF16), (1, 0, 2))


def _row_tile(t, cap):
    tm = min(t, cap)
    while t % tm:
        tm //= 2
    return tm


def kernel(x_prompt, x_sample, cache_sb, cache_nsa, state_win, state_delta, state_conv, page_table, w_in, conv_w,
           a_log, dt_bias, b_out_norm, a_out_norm, c_q_norm, c_k_norm, c_out_norm, w_out, attn_norm, ffn_norm,
           dense_w_gu, dense_w_down, router_w, router_b, moe_w_gu, moe_w_down):
    batch, seq, _ = x_prompt.shape
    n_dec = x_sample.shape[0]
    depth = w_in.shape[0]
    n_phys, _, page = cache_sb.shape[:3]
    n_buf = state_win.shape[2]
    tp, ts = batch * seq, n_dec
    tm_p, tm_s = _row_tile(tp, 512), _row_tile(ts, 512)

    cache_sb2 = jnp.transpose(cache_sb, (0, 1, 3, 4, 5, 2)).reshape(n_phys, depth, 2 * W_A, page)
    cache_nsa2 = jnp.transpose(cache_nsa, (0, 1, 3, 4, 2)).reshape(n_phys, depth, 4 * DH_C, page)
    state_win2 = jnp.transpose(state_win, (0, 1, 3, 4, 2)).reshape(n_dec, depth, 2 * DH_C, n_buf)
    d_ff = dense_w_down.shape[1]
    tf = d_ff // 2 if (d_ff // 2) % LANES == 0 else d_ff

    xp = x_prompt.reshape(tp, D_MODEL)
    xs = x_sample.reshape(ts, D_MODEL)
    outs_p = {k: [] for k in ("sb", "nsa", "win", "delta", "conv")}
    outs_s = {k: [] for k in ("sb", "nsa", "win", "delta", "conv")}
    w_pk_all = _pack_w_in(w_in)
    for l in range(depth):
        w_pk = w_pk_all[l]
        wo = w_out[l].astype(BF16)
        g_attn = attn_norm[l].reshape(1, D_MODEL)
        g_ffn = ffn_norm[l].reshape(1, D_MODEL)
        qn = jnp.tile(c_q_norm[l], H_C).reshape(1, W_C)
        ones64 = jnp.ones((DH_C,), F32)
        kn = jnp.concatenate([ones64, ones64, c_k_norm[l, 1], ones64, c_k_norm[l, 2], ones64]).reshape(1, 384)
        kn0 = jnp.concatenate([c_k_norm[l, 0], ones64]).reshape(1, LANES)
        par = (jnp.zeros((2, LANES), F32).at[0, SMALL_DECAY:SMALL_DECAY + H_B].set(a_log[l])
               .at[1, SMALL_DECAY:SMALL_DECAY + H_B].set(dt_bias[l]))
        gn_a = a_out_norm[l].reshape(1, W_A)
        gn_b = b_out_norm[l].reshape(1, DV_B)
        gn_c = c_out_norm[l].reshape(1, W_C)
        cw = conv_w[l]

        sb, aq, akv, braw, bz, small, _, nsa, win, ckv, cqt, ckvt = _inproj(xp, g_attn, w_pk, qn, kn, tm_p)
        oa = _sb_prompt(aq, akv, gn_a, batch, seq)
        ob, s_fin = _dn_prompt(braw, bz, small, cw, par, gn_b, batch, seq)
        cmp = _nsa_cmp(nsa, kn0, batch, seq)
        oc = _nsa_prompt(cqt, small, cmp, ckv, ckvt, c_out_norm[l].reshape(H_C, DH_C).T, batch, seq)
        outs_p["sb"].append(sb.reshape(batch, seq, 2, H_A, DH_A))
        outs_p["nsa"].append(nsa.reshape(batch, seq, 4, DH_C))
        outs_p["win"].append(win.reshape(batch, seq, 2, DH_C)[:, seq - min(WINDOW, seq):])
        outs_p["delta"].append(s_fin)
        outs_p["conv"].append(braw.reshape(batch, seq, CONV_CH)[:, seq - (CONV_W - 1):])

        sb_s, aq_s, _, braw_s, bz_s, small_s, cq_s, nsa_s, win_s, _, _, _ = _inproj(xs, g_attn, w_pk, qn, kn, tm_s)
        oa_s = _sb_decode(page_table, aq_s.astype(F32), gn_a, cache_sb2, l).astype(BF16)
        ob_s, s_new = _dn_decode(braw_s, bz_s, small_s, state_conv, state_delta, cw, par, gn_b, l)
        oc_s = _nsa_decode(page_table, cq_s.astype(F32), small_s, nsa_s, win_s, state_win2,
                           c_k_norm[l, 0].reshape(DH_C, 1), gn_c, cache_nsa2, l).astype(BF16)
        outs_s["sb"].append(sb_s.reshape(n_dec, 1, 2, H_A, DH_A))
        outs_s["nsa"].append(nsa_s.reshape(n_dec, 1, 4, DH_C))
        win_all = jnp.concatenate([state_win[:, l], win_s.reshape(n_dec, 1, 2, DH_C)], axis=1)
        outs_s["win"].append(win_all[:, win_all.shape[1] - n_buf:])
        outs_s["delta"].append(s_new)
        outs_s["conv"].append(jnp.concatenate([state_conv[:, l], braw_s.reshape(n_dec, 1, CONV_CH)], axis=1)[:, 1:])

        if l % 2 == 0:
            wgu = dense_w_gu[l // 2].astype(BF16)
            wd = dense_w_down[l // 2].astype(BF16)
            xp = _ffn_dense(xp, oa, ob, oc, wo, g_ffn, wgu, wd, tm_p, tf)
            xs = _ffn_dense(xs, oa_s, ob_s, oc_s, wo, g_ffn, wgu, wd, tm_s, tf)
        else:
            wr = jnp.pad(router_w[l // 2], ((0, 0), (0, LANES - N_EXPERTS)))
            br = jnp.pad(router_b[l // 2], (0, LANES - N_EXPERTS)).reshape(1, LANES)
            wgu = moe_w_gu[l // 2].astype(BF16)
            wd = moe_w_down[l // 2].astype(BF16)
            xp = _ffn_moe(xp, oa, ob, oc, wo, g_ffn, wr, br, wgu, wd, tm_p)
            xs = _ffn_moe(xs, oa_s, ob_s, oc_s, wo, g_ffn, wr, br, wgu, wd, tm_s)

    stack = lambda xs_: jnp.stack(xs_, axis=1)
    return (xp.reshape(batch, seq, D_MODEL), xs.reshape(n_dec, 1, D_MODEL),
            stack(outs_p["sb"]), stack(outs_s["sb"]), stack(outs_p["nsa"]), stack(outs_s["nsa"]),
            stack(outs_p["win"]), stack(outs_s["win"]), stack(outs_p["delta"]), stack(outs_s["delta"]),
            stack(outs_p["conv"]), stack(outs_s["conv"]))
```

```python
import functools

import jax
import jax.numpy as jnp
from jax import lax
from jax.experimental import pallas as pl
from jax.experimental.pallas import tpu as pltpu

F32 = jnp.float32
BF16 = jnp.bfloat16

D_MODEL = 1024
H_A, DH_A = 4, 64
H_B, DK_B, DV_B = 4, 128, 128
H_C, DH_C = 4, 64
W_A = H_A * DH_A
W_B = H_B * DV_B
W_C = H_C * DH_C
CONV_W = 4
CONV_CH = 2 * H_B * DK_B + W_B
DN_CHUNK = 64
CMP_BLOCK = 64
SEL_BLOCK = 64
TOP_K_BLOCKS = 16
WINDOW = 512
N_EXPERTS = 8
EPS = 1e-6
NEG_INF = -1e30

LANES = 128
SUBLANES = 8
VMEM_LIMIT = 56 * 1024 * 1024

N_PROJ = 512 + 256 + CONV_CH + W_B + W_C + 256 + 128 + 128
SMALL_BETA, SMALL_DECAY, SMALL_GATE = 0, H_B, 2 * H_B
TQ = 128
SB_TK = 256
NSA_TK = 256
LOG2E = 1.4426950408889634
DN_BLOCK = 256


def _cparams(sem):
    return pltpu.CompilerParams(dimension_semantics=sem, vmem_limit_bytes=VMEM_LIMIT)


def _split2(x):
    hi = x.astype(BF16)
    lo = (x - hi.astype(F32)).astype(BF16)
    return hi, lo


def _split3(x):
    hi = x.astype(BF16)
    r = x - hi.astype(F32)
    mid = r.astype(BF16)
    lo = (r - mid.astype(F32)).astype(BF16)
    return hi, mid, lo


_NN = (((1,), (0,)), ((), ()))
_NT = (((1,), (1,)), ((), ()))
_TN = (((0,), (0,)), ((), ()))


def _mm(a, b, dims=_NN):
    return lax.dot_general(a, b, dims, preferred_element_type=F32)


def _mm_x01(x, m01, parts=2):
    ps = _split2(x) if parts == 2 else _split3(x)
    out = _mm(ps[0], m01)
    for p in ps[1:]:
        out = out + _mm(p, m01)
    return out


def _mm_01x(m01, x, parts=2):
    ps = _split2(x) if parts == 2 else _split3(x)
    out = _mm(m01, ps[0])
    for p in ps[1:]:
        out = out + _mm(m01, p)
    return out


def _mm3(a, b, dims=_NN):
    ah, al = _split2(a)
    bh, bl = _split2(b)
    return _mm(ah, bh, dims) + (_mm(ah, bl, dims) + _mm(al, bh, dims))


def _silu(x):
    return x * (1.0 / (1.0 + jnp.exp(-x)))


def _sigmoid(x):
    return 1.0 / (1.0 + jnp.exp(-x))


def _softplus(x):
    return jnp.maximum(x, 0.0) + jnp.log(1.0 + jnp.exp(-jnp.abs(x)))


def _group_ones(n, group):
    r = lax.broadcasted_iota(jnp.int32, (n, n), 0) // group
    c = lax.broadcasted_iota(jnp.int32, (n, n), 1) // group
    return (r == c).astype(BF16)


def _inproj_kernel(x_ref, g_ref, w_ref, qn_ref, kn_ref,
                   sb_ref, aq_ref, akv_ref, braw_ref, bz_ref, small_ref, cq_ref, nsa_ref, win_ref, ckv_ref,
                   cqt_ref, ckvt_ref, sbt_ref, nsat_ref):
    x = x_ref[...]
    h = x * lax.rsqrt(jnp.mean(x * x, axis=-1, keepdims=True) + EPS) * g_ref[...]
    p = _mm(h.astype(BF16), w_ref[...], _NT)
    o = 0
    sb = p[:, o:o + 512]; o += 512
    aq = p[:, o:o + 256]; o += 256
    braw = p[:, o:o + CONV_CH]; o += CONV_CH
    bz = p[:, o:o + W_B]; o += W_B
    cq = p[:, o:o + W_C]; o += W_C
    nsa = p[:, o:o + 256]; o += 256
    win = p[:, o:o + 128]; o += 128
    small = p[:, o:o + 128]
    sb_ref[...] = sb
    aq_ref[...] = (aq * (DH_A ** -0.5 * LOG2E)).astype(BF16)
    akv_ref[...] = sb.astype(BF16)
    braw_ref[...] = braw
    bz_ref[...] = bz
    small_ref[...] = small
    g64 = _group_ones(256, DH_C)
    ms = _mm_x01(cq * cq, g64) * (1.0 / DH_C)
    cqn = cq * lax.rsqrt(ms + EPS) * qn_ref[...]
    cq_ref[...] = (cqn * (DH_C ** -0.5 * LOG2E)).astype(BF16)
    lane = lax.broadcasted_iota(jnp.int32, nsa.shape, 1)
    is_ks = (lane >= 128) & (lane < 192)
    ms = _mm_x01(nsa * nsa, g64) * (1.0 / DH_C)
    nsa_n = jnp.where(is_ks, nsa * lax.rsqrt(ms + EPS) * kn_ref[:, 0:256], nsa)
    nsa_ref[...] = nsa_n
    lane = lax.broadcasted_iota(jnp.int32, win.shape, 1)
    ms = _mm_x01(win * win, g64[0:128, 0:128]) * (1.0 / DH_C)
    win_n = jnp.where(lane < 64, win * lax.rsqrt(ms + EPS) * kn_ref[:, 256:384], win)
    win_ref[...] = win_n
    ckv = jnp.concatenate([nsa_n[:, 128:256], win_n], axis=1)
    ckv_ref[...] = ckv.astype(BF16)
    cqt_ref[...] = (cqn * (DH_C ** -0.5 * LOG2E)).T.astype(BF16)
    ckvt_ref[...] = ckv.T.astype(BF16)
    sbt_ref[...] = sb.T
    nsat_ref[...] = nsa_n.T


def _inproj(x, g, w, qn, kn, tm):
    t = x.shape[0]
    row = lambda i: (i, 0)
    const = lambda i: (0, 0)
    outs = [(512, F32), (256, BF16), (512, BF16), (CONV_CH, F32), (W_B, F32), (128, F32),
            (W_C, BF16), (256, F32), (128, F32), (256, BF16)]
    return pl.pallas_call(
        _inproj_kernel,
        grid=(t // tm,),
        in_specs=[pl.BlockSpec((tm, D_MODEL), row), pl.BlockSpec((1, D_MODEL), const),
                  pl.BlockSpec((N_PROJ, D_MODEL), const), pl.BlockSpec((1, 256), const),
                  pl.BlockSpec((1, 384), const)],
        out_specs=[pl.BlockSpec((tm, n), row) for n, _ in outs]
        + [pl.BlockSpec((n, tm), lambda i: (0, i)) for n in (W_C, 4 * DH_C, 2 * W_A, 4 * DH_C)],
        out_shape=[jax.ShapeDtypeStruct((t, n), d) for n, d in outs]
        + [jax.ShapeDtypeStruct((W_C, t), BF16), jax.ShapeDtypeStruct((4 * DH_C, t), BF16),
           jax.ShapeDtypeStruct((2 * W_A, t), F32), jax.ShapeDtypeStruct((4 * DH_C, t), F32)],
        compiler_params=_cparams(("parallel",)),
        name="inproj",
    )(x, g, w, qn, kn)


def _mix_prologue(x_ref, oa_ref, ob_ref, oc_ref, wo_ref, g_ref, xn_scr, h_scr):
    mix = (_mm(oa_ref[...], wo_ref[0:W_A, :]) + _mm(ob_ref[...], wo_ref[W_A:W_A + W_B, :])
           + _mm(oc_ref[...], wo_ref[W_A + W_B:, :]))
    xn = x_ref[...] + mix
    xn_scr[...] = xn
    h = xn * lax.rsqrt(jnp.mean(xn * xn, axis=-1, keepdims=True) + EPS) * g_ref[...]
    h_scr[...] = h.astype(BF16)
    return h


def _ffn_dense_kernel(x_ref, oa_ref, ob_ref, oc_ref, wo_ref, g_ref, wg_ref, wu_ref, wd_ref, o_ref,
                      xn_scr, h_scr, acc_scr):
    f = pl.program_id(1)

    @pl.when(f == 0)
    def _():
        _mix_prologue(x_ref, oa_ref, ob_ref, oc_ref, wo_ref, g_ref, xn_scr, h_scr)
        acc_scr[...] = jnp.zeros_like(acc_scr)

    h = h_scr[...]
    act = _silu(_mm(h, wg_ref[...])) * _mm(h, wu_ref[...])
    acc_scr[...] += _mm(act.astype(BF16), wd_ref[...])

    @pl.when(f == pl.num_programs(1) - 1)
    def _():
        o_ref[...] = xn_scr[...] + acc_scr[...]


def _ffn_dense(x, oa, ob, oc, wo, g, wgu, wd, tm, tf):
    t = x.shape[0]
    d_ff = wd.shape[0]
    nf = d_ff // tf
    row = lambda i, f: (i, 0)
    const = lambda i, f: (0, 0)
    return pl.pallas_call(
        _ffn_dense_kernel,
        grid=(t // tm, nf),
        in_specs=[pl.BlockSpec((tm, D_MODEL), row), pl.BlockSpec((tm, W_A), row), pl.BlockSpec((tm, W_B), row),
                  pl.BlockSpec((tm, W_C), row), pl.BlockSpec((D_MODEL, D_MODEL), const),
                  pl.BlockSpec((1, D_MODEL), const),
                  pl.BlockSpec((D_MODEL, tf), lambda i, f: (0, f)),
                  pl.BlockSpec((D_MODEL, tf), lambda i, f: (0, f + nf)),
                  pl.BlockSpec((tf, D_MODEL), lambda i, f: (f, 0))],
        out_specs=pl.BlockSpec((tm, D_MODEL), row),
        out_shape=jax.ShapeDtypeStruct((t, D_MODEL), F32),
        scratch_shapes=[pltpu.VMEM((tm, D_MODEL), F32), pltpu.VMEM((tm, D_MODEL), BF16),
                        pltpu.VMEM((tm, D_MODEL), F32)],
        compiler_params=_cparams(("parallel", "arbitrary")),
        name="ffn_dense",
    )(x, oa, ob, oc, wo, g, wgu, wgu, wd)


def _ffn_moe_kernel(x_ref, oa_ref, ob_ref, oc_ref, wo_ref, g_ref, wr_ref, br_ref, wg_ref, wu_ref, wd_ref, o_ref,
                    xn_scr, h_scr, acc_scr, gate_scr):
    e = pl.program_id(1)

    @pl.when(e == 0)
    def _():
        h = _mix_prologue(x_ref, oa_ref, ob_ref, oc_ref, wo_ref, g_ref, xn_scr, h_scr)
        acc_scr[...] = jnp.zeros_like(acc_scr)
        logits = _mm3(h, wr_ref[...]) + br_ref[...]
        lane = lax.broadcasted_iota(jnp.int32, logits.shape, 1)
        logits = jnp.where(lane < N_EXPERTS, logits, -jnp.inf)
        m1 = jnp.max(logits, axis=-1, keepdims=True)
        i1 = jnp.min(jnp.where(logits == m1, lane, LANES), axis=-1, keepdims=True)
        rest = jnp.where(lane == i1, -jnp.inf, logits)
        m2 = jnp.max(rest, axis=-1, keepdims=True)
        i2 = jnp.min(jnp.where(rest == m2, lane, LANES), axis=-1, keepdims=True)
        e2 = jnp.exp(m2 - m1)
        den = 1.0 + e2
        gate_scr[...] = jnp.where(lane == i1, 1.0 / den, 0.0) + jnp.where(lane == i2, e2 / den, 0.0)

    h = h_scr[...]
    act = _silu(_mm(h, wg_ref[...])) * _mm(h, wu_ref[...])
    y = _mm(act.astype(BF16), wd_ref[...])
    lane = lax.broadcasted_iota(jnp.int32, gate_scr.shape, 1)
    ge = jnp.sum(jnp.where(lane == e, gate_scr[...], 0.0), axis=-1, keepdims=True)
    acc_scr[...] += ge * y

    @pl.when(e == pl.num_programs(1) - 1)
    def _():
        o_ref[...] = xn_scr[...] + acc_scr[...]


def _ffn_moe(x, oa, ob, oc, wo, g, wr, br, wgu, wd, tm):
    t = x.shape[0]
    ne, _, two_f = wgu.shape
    f = two_f // 2
    row = lambda i, e: (i, 0)
    const = lambda i, e: (0, 0)
    return pl.pallas_call(
        _ffn_moe_kernel,
        grid=(t // tm, ne),
        in_specs=[pl.BlockSpec((tm, D_MODEL), row), pl.BlockSpec((tm, W_A), row), pl.BlockSpec((tm, W_B), row),
                  pl.BlockSpec((tm, W_C), row), pl.BlockSpec((D_MODEL, D_MODEL), const),
                  pl.BlockSpec((1, D_MODEL), const), pl.BlockSpec((D_MODEL, LANES), const),
                  pl.BlockSpec((1, LANES), const),
                  pl.BlockSpec((None, D_MODEL, f), lambda i, e: (e, 0, 0)),
                  pl.BlockSpec((None, D_MODEL, f), lambda i, e: (e, 0, 1)),
                  pl.BlockSpec((None, f, D_MODEL), lambda i, e: (e, 0, 0))],
        out_specs=pl.BlockSpec((tm, D_MODEL), row),
        out_shape=jax.ShapeDtypeStruct((t, D_MODEL), F32),
        scratch_shapes=[pltpu.VMEM((tm, D_MODEL), F32), pltpu.VMEM((tm, D_MODEL), BF16),
                        pltpu.VMEM((tm, D_MODEL), F32), pltpu.VMEM((tm, LANES), F32)],
        compiler_params=_cparams(("parallel", "arbitrary")),
        name="ffn_moe",
    )(x, oa, ob, oc, wo, g, wr, br, wgu, wgu, wd)


def _sb_log2_terms(z2):
    neg_abs = lax.bitcast_convert_type(lax.bitcast_convert_type(z2, jnp.int32) | jnp.int32(-2 ** 31), F32)
    sp = jnp.log2(1.0 + jnp.exp2(neg_abs))
    ls = jnp.minimum(z2, 0.0) - sp
    return ls, ls - z2


def _sb_prompt_kernel(q_ref, kv_ref, gn_ref, o_ref):
    i = pl.program_id(1)
    tk = SB_TK
    nr = H_A * TQ
    r = lax.broadcasted_iota(jnp.int32, (tk, tk), 0)
    c = lax.broadcasted_iota(jnp.int32, (tk, tk), 1)
    later01 = (r > c).astype(BF16)
    t_row = i * TQ + lax.broadcasted_iota(jnp.int32, (nr, 1), 0) % TQ
    s_off = lax.broadcasted_iota(jnp.int32, (1, tk), 1)
    qs = [q_ref[:, h * DH_A:(h + 1) * DH_A] for h in range(H_A)]

    def terms(j, diag):
        start = pl.multiple_of(j * tk, tk)
        z2 = jnp.concatenate([_mm(qs[h], kv_ref[pl.ds(start, tk), h * DH_A:(h + 1) * DH_A], _NT)
                              for h in range(H_A)], axis=0)
        ls, lk = _sb_log2_terms(z2)
        before = (start + s_off) < t_row if diag else None
        if diag:
            lk = jnp.where(before, lk, 0.0)
        return ls + _mm(lk.astype(BF16), later01), jnp.sum(lk, axis=-1, keepdims=True), before

    def weighted_values(j, logw, before):
        start = pl.multiple_of(j * tk, tk)
        w = jnp.exp2(logw)
        if before is not None:
            w = jnp.where(before, w, 0.0)
        wh = w.astype(BF16)
        return jnp.concatenate([_mm(wh[h * TQ:(h + 1) * TQ],
                                    kv_ref[pl.ds(start, tk), W_A + h * DH_A:W_A + (h + 1) * DH_A])
                                for h in range(H_A)], axis=0)

    def step(j, carry, diag):
        run, acc = carry
        lw, tot, before = terms(j, diag)
        return run + tot, acc + weighted_values(j, lw + run, before)

    def two_steps(j, carry):
        run, acc = carry
        lw_a, tot_a, _ = terms(j, False)
        lw_b, tot_b, _ = terms(j - 1, False)
        run_b = run + tot_a
        acc = acc + weighted_values(j, lw_a + run, None) + weighted_values(j - 1, lw_b + run_b, None)
        return run_b + tot_b, acc

    jd = i // (tk // TQ)
    carry = (jnp.zeros((nr, 1), F32), jnp.zeros((nr, DH_A), F32))
    carry = step(jd, carry, True)
    carry = lax.fori_loop(0, jd // 2, lambda p, cr: two_steps(jd - 1 - 2 * p, cr), carry)
    carry = lax.cond(jd % 2 == 1, lambda cr: step(0, cr, False), lambda cr: cr, carry)
    o = carry[1]
    o = o * lax.rsqrt(jnp.mean(o * o, axis=-1, keepdims=True) + EPS)
    o = jnp.concatenate([o[h * TQ:(h + 1) * TQ] for h in range(H_A)], axis=1)
    o_ref[...] = (o * gn_ref[...]).astype(BF16)


def _sb_prompt(aq, akv, gn, batch, seq):
    nq = seq // TQ
    return pl.pallas_call(
        _sb_prompt_kernel,
        grid=(batch, nq),
        in_specs=[pl.BlockSpec((TQ, W_A), lambda b, i: (b * nq + i, 0)),
                  pl.BlockSpec((seq, 2 * W_A), lambda b, i: (b, 0)),
                  pl.BlockSpec((1, W_A), lambda b, i: (0, 0))],
        out_specs=pl.BlockSpec((TQ, W_A), lambda b, i: (b * nq + i, 0)),
        out_shape=jax.ShapeDtypeStruct((batch * seq, W_A), BF16),
        compiler_params=_cparams(("parallel", "arbitrary")),
        name="sb_prompt",
    )(aq, akv, gn)


SB_DEC_GROUP = 2
NSA_DEC_GROUP = 4


def _page_copies(pt_ref, cache_ref, buf_ref, sem_ref, layer, step, slot, n_pages, page):
    group = buf_ref.shape[1]
    return [pltpu.make_async_copy(cache_ref.at[pt_ref[step * group + g, p], layer],
                                  buf_ref.at[slot, g, :, pl.ds(p * page, page)], sem_ref.at[slot])
            for g in range(group) for p in range(n_pages)]


def _interleave(stage_generators):
    live = list(stage_generators)
    while live:
        live = [gen for gen in live if next(gen, _DONE) is not _DONE]


_DONE = object()


def _paged_prefetch(pt_ref, cache_ref, buf_ref, sem_ref, layer, n_pages, page):
    s = pl.program_id(0)
    slot = s % 2

    @pl.when(s == 0)
    def _():
        for cp in _page_copies(pt_ref, cache_ref, buf_ref, sem_ref, layer, 0, 0, n_pages, page):
            cp.start()

    @pl.when(s + 1 < pl.num_programs(0))
    def _():
        for cp in _page_copies(pt_ref, cache_ref, buf_ref, sem_ref, layer, s + 1, 1 - slot, n_pages, page):
            cp.start()

    for cp in _page_copies(pt_ref, cache_ref, buf_ref, sem_ref, layer, s, slot, n_pages, page):
        cp.wait()
    return slot


def _chunk_rows(x, n_chunks):
    return jnp.concatenate([x[:, c * LANES:(c + 1) * LANES] for c in range(n_chunks)], axis=0)


def _chunk_lanes(x, n_chunks):
    return jnp.concatenate([x[c * SUBLANES:(c + 1) * SUBLANES] for c in range(n_chunks)], axis=1)


def _sb_decode_kernel(pt_ref, q_ref, gn_ref, cache_ref, o_ref, buf_ref, sem_ref, *, layer, n_pages, page):
    slot = _paged_prefetch(pt_ref, cache_ref, buf_ref, sem_ref, layer, n_pages, page)
    group = buf_ref.shape[1]
    hr = SUBLANES
    nc = n_pages * page // LANES
    rr = lax.broadcasted_iota(jnp.int32, (hr, W_A), 0)
    cc = lax.broadcasted_iota(jnp.int32, (hr, W_A), 1)
    own = rr == cc // DH_A
    r = lax.broadcasted_iota(jnp.int32, (LANES, LANES), 0)
    c = lax.broadcasted_iota(jnp.int32, (LANES, LANES), 1)
    later01 = (r > c).astype(BF16)
    r = lax.broadcasted_iota(jnp.int32, (hr * nc, hr * nc), 0)
    c = lax.broadcasted_iota(jnp.int32, (hr * nc, hr * nc), 1)
    later_chunk01 = ((r % hr == c % hr) & (c // hr > r // hr)).astype(BF16)
    group1 = _group_ones(W_A, DH_A)

    def one_sequence(g):
        s = pl.program_id(0) * group + g
        q = q_ref[pl.ds(s, 1), :]
        q8 = jnp.where(own, q, 0.0).astype(BF16)
        z2 = _mm(q8, buf_ref[slot, g, 0:W_A, :].astype(BF16))
        yield
        ls, lk = _sb_log2_terms(z2)
        ls_c, lk_c = _chunk_rows(ls, nc), _chunk_rows(lk, nc)
        inner = _mm_x01(lk_c, later01)
        tot = jnp.broadcast_to(jnp.sum(lk_c, axis=-1, keepdims=True), (hr * nc, LANES))
        yield
        outer = _mm_01x(later_chunk01, tot)
        w = _chunk_lanes(jnp.exp2(ls_c + inner + outer), nc)
        yield
        o8 = _mm(w.astype(BF16), buf_ref[slot, g, W_A:2 * W_A, :].astype(BF16), _NT)
        yield
        o = jnp.sum(jnp.where(own, o8, 0.0), axis=0, keepdims=True)
        ms = _mm_x01(o * o, group1) * (1.0 / DH_A)
        o_ref[pl.ds(s, 1), :] = o * lax.rsqrt(ms + EPS) * gn_ref[...]

    _interleave([one_sequence(g) for g in range(group)])


def _sb_decode(page_table, aq, gn, cache, layer):
    n_seq, n_pages = page_table.shape
    page = cache.shape[3]
    kern = functools.partial(_sb_decode_kernel, layer=layer, n_pages=n_pages, page=page)
    return pl.pallas_call(
        kern,
        grid_spec=pltpu.PrefetchScalarGridSpec(
            num_scalar_prefetch=1,
            grid=(n_seq // SB_DEC_GROUP,),
            in_specs=[pl.BlockSpec((n_seq, W_A), lambda s, pt: (0, 0)),
                      pl.BlockSpec((1, W_A), lambda s, pt: (0, 0)),
                      pl.BlockSpec(memory_space=pl.ANY)],
            out_specs=pl.BlockSpec((n_seq, W_A), lambda s, pt: (0, 0)),
            scratch_shapes=[pltpu.VMEM((2, SB_DEC_GROUP, 2 * W_A, n_pages * page), F32),
                            pltpu.SemaphoreType.DMA((2,))]),
        out_shape=jax.ShapeDtypeStruct((n_seq, W_A), F32),
        compiler_params=_cparams(("arbitrary",)),
        name="sb_decode",
    )(page_table, aq, gn, cache)


def _unit_lower_inverse(a_mats, n):
    r = lax.broadcasted_iota(jnp.int32, (n, n), 0)
    c = lax.broadcasted_iota(jnp.int32, (n, n), 1)
    eye = (r == c).astype(F32)
    base = 16
    mm1 = lambda a, b: _mm(a.astype(BF16), b.astype(BF16))
    xs = [jnp.where(r // base == c // base, -a, 0.0) for a in a_mats]
    invs = [eye + x for x in xs]
    span = 1
    while 2 * span < base:
        xs = [mm1(x, x) for x in xs]
        invs = [inv + mm1(inv, x) for inv, x in zip(invs, xs)]
        span *= 2
    size = base
    while size < DN_CHUNK:
        off_mask = (r // (2 * size) == c // (2 * size)) & (r // size != c // size)
        mids = [mm1(jnp.where(off_mask, a, 0.0), inv) for a, inv in zip(a_mats, invs)]
        invs = [inv - mm1(inv, mid) for inv, mid in zip(invs, mids)]
        size *= 2
    return invs


def _l2n(x):
    return x * lax.rsqrt(jnp.sum(x * x, axis=-1, keepdims=True) + EPS)


def _dn_prompt_kernel(raw_ref, z_ref, small_ref, cw_ref, par_ref, gn_ref, o_ref, s_ref, xbuf, s_scr):
    ci = pl.program_id(1)
    tb = DN_BLOCK
    nk = H_B * DK_B

    @pl.when(ci == 0)
    def _():
        xbuf[0:SUBLANES, :] = jnp.zeros((SUBLANES, CONV_CH), F32)
        s_scr[...] = jnp.zeros_like(s_scr)

    xbuf[SUBLANES:SUBLANES + tb, :] = raw_ref[...]
    conv = xbuf[SUBLANES - 3:SUBLANES - 3 + tb, :] * cw_ref[0:1, :]
    for i in range(1, CONV_W):
        conv = conv + xbuf[SUBLANES - 3 + i:SUBLANES - 3 + i + tb, :] * cw_ref[i:i + 1, :]
    xbuf[0:SUBLANES, :] = xbuf[tb:tb + SUBLANES, :]
    qkv = _silu(conv)

    small = small_ref[...]
    beta_all = _sigmoid(small)
    g_all = -jnp.exp(par_ref[0:1, :]) * _softplus(small + par_ref[1:2, :])

    r = lax.broadcasted_iota(jnp.int32, (tb, tb), 0)
    c = lax.broadcasted_iota(jnp.int32, (tb, tb), 1)
    same = (r // DN_CHUNK) == (c // DN_CHUNK)
    lower = same & (r >= c)
    strict = same & (r > c)
    lower01 = lower.astype(BF16)
    upper01 = (same & (r <= c)).astype(BF16)
    same01 = same.astype(BF16)
    ones8 = jnp.ones((SUBLANES, tb), BF16)

    g_cum = _mm_01x(lower01, g_all, parts=3)
    g_tot = _mm_01x(same01, g_all, parts=3)
    e_cum = jnp.exp(g_cum)
    e_rest = jnp.exp(g_tot - g_cum)
    e_tot = jnp.exp(g_tot)

    heads = range(H_B)
    lanes = [SMALL_DECAY + h for h in heads]
    qs = [_l2n(qkv[:, h * DK_B:(h + 1) * DK_B]) * (DK_B ** -0.5) for h in heads]
    ks = [_l2n(qkv[:, nk + h * DK_B:nk + (h + 1) * DK_B]) for h in heads]
    vs = [qkv[:, 2 * nk + h * DV_B:2 * nk + (h + 1) * DV_B] for h in heads]
    betas = [beta_all[:, SMALL_BETA + h:SMALL_BETA + h + 1] for h in heads]
    upper_f = upper01.astype(F32)
    gc_rows = [_mm_01x(ones8, g_all[:, ln:ln + 1] * upper_f, parts=3)[0:1, :] for ln in lanes]
    decays = [jnp.exp(jnp.where(lower, g_cum[:, ln:ln + 1] - gr, -jnp.inf)) for ln, gr in zip(lanes, gc_rows)]
    kbs = [k * b for k, b in zip(ks, betas)]
    khs = [k.astype(BF16) for k in ks]
    qhs = [q.astype(BF16) for q in qs]
    a_mats = [jnp.where(strict, _mm(kb.astype(BF16), kh, _NT) * d, 0.0) for kb, kh, d in zip(kbs, khs, decays)]
    invs = _unit_lower_inverse(a_mats, tb)
    sols = [_mm3(inv, jnp.concatenate([v * b, kb * e_cum[:, ln:ln + 1]], axis=1))
            for inv, v, b, kb, ln in zip(invs, vs, betas, kbs, lanes)]
    u_bases = [sol[:, 0:DV_B] for sol in sols]
    whs = [sol[:, DV_B:].astype(BF16) for sol in sols]
    qkhs = [(_mm(qh, kh, _NT) * d).astype(BF16) for qh, kh, d in zip(qhs, khs, decays)]
    q_decs = [(q * e_cum[:, ln:ln + 1]).astype(BF16) for q, ln in zip(qs, lanes)]
    k_decs = [(k * e_rest[:, ln:ln + 1]).astype(BF16) for k, ln in zip(ks, lanes)]
    states = [s_scr[h] for h in heads]
    outs = [[] for _ in heads]
    for n in range(tb // DN_CHUNK):
        lo, hi = n * DN_CHUNK, (n + 1) * DN_CHUNK
        shs = [st.astype(BF16) for st in states]
        us = [ub[lo:hi] - _mm(wh[lo:hi], sh) for ub, wh, sh in zip(u_bases, whs, shs)]
        uhs = [u.astype(BF16) for u in us]
        for h in heads:
            outs[h].append(_mm(q_decs[h][lo:hi], shs[h]) + _mm(qkhs[h][lo:hi, lo:hi], uhs[h]))
        states = [st * e_tot[lo:lo + 1, ln:ln + 1] + _mm(kd[lo:hi], uh, _TN)
                  for st, ln, kd, uh in zip(states, lanes, k_decs, uhs)]
    for h in heads:
        s_scr[h] = states[h]
        o = jnp.concatenate(outs[h], axis=0)
        o = o * lax.rsqrt(jnp.mean(o * o, axis=-1, keepdims=True) + EPS) * gn_ref[...]
        o_ref[:, h * DV_B:(h + 1) * DV_B] = (o * _silu(z_ref[:, h * DV_B:(h + 1) * DV_B])).astype(BF16)

    @pl.when(ci == pl.num_programs(1) - 1)
    def _():
        s_ref[...] = s_scr[...]


def _dn_prompt(braw, bz, small, cw, par, gn, batch, seq):
    tb = DN_BLOCK
    nb = seq // tb
    row = lambda b, i: (b * nb + i, 0)
    const = lambda b, i: (0, 0)
    return pl.pallas_call(
        _dn_prompt_kernel,
        grid=(batch, nb),
        in_specs=[pl.BlockSpec((tb, CONV_CH), row), pl.BlockSpec((tb, W_B), row), pl.BlockSpec((tb, LANES), row),
                  pl.BlockSpec((CONV_W, CONV_CH), const), pl.BlockSpec((2, LANES), const),
                  pl.BlockSpec((1, DV_B), const)],
        out_specs=[pl.BlockSpec((tb, W_B), row),
                   pl.BlockSpec((None, H_B, DK_B, DV_B), lambda b, i: (b, 0, 0, 0))],
        out_shape=[jax.ShapeDtypeStruct((batch * seq, W_B), BF16),
                   jax.ShapeDtypeStruct((batch, H_B, DK_B, DV_B), F32)],
        scratch_shapes=[pltpu.VMEM((SUBLANES + tb, CONV_CH), F32), pltpu.VMEM((H_B, DK_B, DV_B), F32)],
        compiler_params=_cparams(("parallel", "arbitrary")),
        name="dn_prompt",
    )(braw, bz, small, cw, par, gn)


DN_DEC_SEQS = 8


def _dn_decode_kernel(raw_ref, z_ref, small_ref, conv_ref, st_ref, cw_ref, par_ref, gn_ref, o_ref, s_ref):
    nk = H_B * DK_B
    conv = raw_ref[...] * cw_ref[CONV_W - 1:CONV_W, :]
    for i in range(CONV_W - 1):
        conv = conv + conv_ref[:, i, :] * cw_ref[i:i + 1, :]
    qkv = _silu(conv)
    small = small_ref[...]
    beta_all = _sigmoid(small)
    eg_all = jnp.exp(-jnp.exp(par_ref[0:1, :]) * _softplus(small + par_ref[1:2, :]))
    zero7 = jnp.zeros((SUBLANES - 1, DK_B), F32)
    outs = [[None] * DN_DEC_SEQS for _ in range(H_B)]

    def one_state(h, s, q, k, v, beta, eg, qk):
        state = st_ref[s, h]
        kq = jnp.concatenate([k[s:s + 1], q[s:s + 1], zero7[0:SUBLANES - 2]], axis=0).astype(BF16)
        proj = _mm(kq, state.astype(BF16))
        yield
        e = eg[s:s + 1]
        u = beta[s:s + 1] * (v[s:s + 1] - e * proj[0:1])
        outs[h][s] = e * proj[1:2] + qk[s:s + 1] * u
        k8 = jnp.concatenate([k[s:s + 1], zero7], axis=0).astype(BF16)
        u8 = jnp.concatenate([u, zero7], axis=0).astype(BF16)
        s_ref[s, h] = state * e + _mm(k8, u8, _TN)

    chains = []
    for h in range(H_B):
        q = _l2n(qkv[:, h * DK_B:(h + 1) * DK_B]) * (DK_B ** -0.5)
        k = _l2n(qkv[:, nk + h * DK_B:nk + (h + 1) * DK_B])
        v = qkv[:, 2 * nk + h * DV_B:2 * nk + (h + 1) * DV_B]
        beta = beta_all[:, SMALL_BETA + h:SMALL_BETA + h + 1]
        eg = eg_all[:, SMALL_DECAY + h:SMALL_DECAY + h + 1]
        qk = jnp.sum(q * k, axis=-1, keepdims=True)
        chains += [one_state(h, s, q, k, v, beta, eg, qk) for s in range(DN_DEC_SEQS)]
    _interleave(chains)
    for h in range(H_B):
        o = jnp.concatenate(outs[h], axis=0)
        o = o * lax.rsqrt(jnp.mean(o * o, axis=-1, keepdims=True) + EPS) * gn_ref[...]
        o_ref[:, h * DV_B:(h + 1) * DV_B] = (o * _silu(z_ref[:, h * DV_B:(h + 1) * DV_B])).astype(BF16)


def _dn_decode(braw, bz, small, state_conv, state_delta, cw, par, gn, layer):
    n_seq = braw.shape[0]
    ts = DN_DEC_SEQS
    row = lambda i: (i, 0)
    const = lambda i: (0, 0)
    return pl.pallas_call(
        _dn_decode_kernel,
        grid=(n_seq // ts,),
        in_specs=[pl.BlockSpec((ts, CONV_CH), row), pl.BlockSpec((ts, W_B), row), pl.BlockSpec((ts, LANES), row),
                  pl.BlockSpec((ts, None, CONV_W - 1, CONV_CH), lambda i: (i, layer, 0, 0)),
                  pl.BlockSpec((ts, None, H_B, DK_B, DV_B), lambda i: (i, layer, 0, 0, 0)),
                  pl.BlockSpec((CONV_W, CONV_CH), const), pl.BlockSpec((2, LANES), const),
                  pl.BlockSpec((1, DV_B), const)],
        out_specs=[pl.BlockSpec((ts, W_B), row),
                   pl.BlockSpec((ts, H_B, DK_B, DV_B), lambda i: (i, 0, 0, 0))],
        out_shape=[jax.ShapeDtypeStruct((n_seq, W_B), BF16),
                   jax.ShapeDtypeStruct((n_seq, H_B, DK_B, DV_B), F32)],
        compiler_params=_cparams(("parallel",)),
        name="dn_decode",
    )(braw, bz, small, state_conv, state_delta, cw, par, gn)


def _block_summaries(rows, kn0):
    n = rows.shape[0]
    m = jnp.mean(rows.reshape(n // CMP_BLOCK, CMP_BLOCK, LANES), axis=1)
    lane = lax.broadcasted_iota(jnp.int32, m.shape, 1)
    is_k = lane < DH_C
    ms = jnp.sum(jnp.where(is_k, m * m, 0.0), axis=-1, keepdims=True) * (1.0 / DH_C)
    return jnp.where(is_k, m * lax.rsqrt(ms + EPS) * kn0, m)


def _nsa_cmp_kernel(rows_ref, kn0_ref, o_ref):
    o_ref[...] = _block_summaries(rows_ref[...], kn0_ref[...]).astype(BF16)


def _nsa_cmp(nsa_rows, kn0, batch, seq):
    nb = seq // CMP_BLOCK
    return pl.pallas_call(
        _nsa_cmp_kernel,
        grid=(batch,),
        in_specs=[pl.BlockSpec((seq, LANES), lambda b: (b, 0)), pl.BlockSpec((1, LANES), lambda b: (0, 0))],
        out_specs=pl.BlockSpec((nb, LANES), lambda b: (b, 0)),
        out_shape=jax.ShapeDtypeStruct((batch * nb, LANES), BF16),
        compiler_params=_cparams(("parallel",)),
        name="nsa_cmp",
    )(nsa_rows, kn0)


def _alibi_slopes2(shape, axis, per_head):
    h = lax.broadcasted_iota(jnp.int32, shape, axis) // per_head
    return jnp.exp2((h + 1).astype(F32) * (-8.0 / H_C)) * LOG2E


def _masked_softmax2(s2, mask, axis):
    s2 = jnp.where(mask, s2, NEG_INF)
    m = jnp.max(s2, axis=axis, keepdims=True)
    e = jnp.where(mask, jnp.exp2(s2 - m), 0.0)
    return e / jnp.maximum(jnp.sum(e, axis=axis, keepdims=True), 1e-30)


def _nsa_prompt_kernel(qt_ref, small_ref, cmp_ref, kv_ref, kvt_ref, gn_ref, o_ref, s_scr, *, n_blocks):
    i = pl.program_id(1)
    nr = H_C * TQ
    tk = NSA_TK
    qt = jnp.concatenate([qt_ref[h * DH_C:(h + 1) * DH_C, :] for h in range(H_C)], axis=1)
    qt_pad = jnp.concatenate([qt, jnp.zeros_like(qt)], axis=0)
    t_col = i * TQ + lax.broadcasted_iota(jnp.int32, (1, nr), 1) % TQ
    t_one = t_col[:, 0:TQ]
    slope_c = _alibi_slopes2((1, nr), 1, TQ)

    kcb = cmp_ref[:, 0:DH_C]
    vcb = cmp_ref[:, DH_C:2 * DH_C]
    blk = lax.broadcasted_iota(jnp.int32, (n_blocks, 1), 0)
    blk_mid = (blk * CMP_BLOCK).astype(F32) + 0.5 * (CMP_BLOCK - 1)
    vis_c = ((blk + 1) * CMP_BLOCK - 1) <= t_col
    s_c = _mm(kcb, qt) - slope_c * (t_col.astype(F32) - blk_mid)
    p_c = _masked_softmax2(s_c, vis_c, 0)
    o_c = _mm(vcb, p_c.astype(BF16), _TN)

    imp = p_c[:, 0:TQ]
    for h in range(1, H_C):
        imp = imp + p_c[:, h * TQ:(h + 1) * TQ]
    cur = t_one // SEL_BLOCK
    jb = lax.broadcasted_iota(jnp.int32, (n_blocks, TQ), 0)
    forced = (jb == 0) | (jb == cur) | (jb == cur - 1)
    score = jnp.where(jb > cur, -1.0, jnp.where(forced, 2.0 * H_C, imp))
    rank = jnp.zeros((n_blocks, TQ), F32)
    for n in range(n_blocks):
        other = score[n:n + 1, :]
        rank = rank + ((other > score) | ((other == score) & (jb > n))).astype(F32)
    sel = ((rank < float(min(TOP_K_BLOCKS, n_blocks))) & (jb <= cur)).astype(BF16)

    bpt = tk // SEL_BLOCK
    er = lax.broadcasted_iota(jnp.int32, (tk, n_blocks), 0) // SEL_BLOCK
    ec = lax.broadcasted_iota(jnp.int32, (tk, n_blocks), 1)
    s_off = lax.broadcasted_iota(jnp.int32, (tk, 1), 0)
    row = lax.broadcasted_iota(jnp.int32, (LANES, tk), 0)

    def attend(j_lo, j_hi, kv_lo, selected):
        def score_tile(j):
            start = pl.multiple_of(j * tk, tk)
            kv = kv_ref[pl.ds(start, tk), kv_lo:kv_lo + LANES]
            s_pos = start + s_off
            valid = s_pos <= t_one
            if selected:
                valid = valid & (_mm((er + j * bpt == ec).astype(BF16), sel) > 0.5)
            else:
                valid = valid & (t_one - s_pos < WINDOW)
            dist = jnp.where(valid, (s_pos - t_one).astype(F32), NEG_INF)
            s2 = _mm(kv, qt_pad) + slope_c * jnp.concatenate([dist] * H_C, axis=1)
            s_scr[j] = s2
            return jnp.max(s2.reshape(tk // SUBLANES, SUBLANES, nr), axis=0)

        p_lo, p_hi = j_lo // 2, (j_hi + 1) // 2

        def scores(p, mx):
            return jnp.maximum(mx, jnp.maximum(score_tile(2 * p), score_tile(2 * p + 1)))

        mx = lax.fori_loop(p_lo, p_hi, scores, jnp.full((SUBLANES, nr), NEG_INF, F32))
        m = jnp.max(mx, axis=0, keepdims=True)

        def weigh_tile(j):
            start = pl.multiple_of(j * tk, tk)
            kvt = kvt_ref[kv_lo:kv_lo + LANES, pl.ds(start, tk)]
            ones_vt = jnp.where(row < DH_C, 1.0, kvt)
            p = jnp.exp2(s_scr[j] - m)
            return _mm(ones_vt, p.astype(BF16))

        def weigh(p, acc):
            return acc + (weigh_tile(2 * p) + weigh_tile(2 * p + 1))

        acc = lax.fori_loop(p_lo, p_hi, weigh, jnp.zeros((LANES, nr), F32))
        return acc[DH_C:2 * DH_C] / acc[0:DH_C]

    j_end = (i * TQ) // tk + 1
    o_s = attend(0, j_end, 0, True)
    j_win = jnp.maximum(i * TQ - (WINDOW - 1), 0) // tk
    o_w = attend(j_win, j_end, 2 * DH_C, False)

    small_t = small_ref[...].T

    def gate_row(rr):
        g = jnp.concatenate([small_t[SMALL_GATE + 3 * h + rr:SMALL_GATE + 3 * h + rr + 1, :]
                             for h in range(H_C)], axis=1)
        return _sigmoid(g)

    o = gate_row(0) * o_c + gate_row(1) * o_s + gate_row(2) * o_w
    o = o * lax.rsqrt(jnp.mean(o * o, axis=0, keepdims=True) + EPS)
    gain = jnp.concatenate([jnp.broadcast_to(gn_ref[:, h:h + 1], (DH_C, TQ)) for h in range(H_C)], axis=1)
    o = o * gain
    pairs = [jnp.concatenate([o[:, (2 * g) * TQ:(2 * g + 1) * TQ], o[:, (2 * g + 1) * TQ:(2 * g + 2) * TQ]], axis=0).T
             for g in range(H_C // 2)]
    o_ref[...] = jnp.concatenate(pairs, axis=1).astype(BF16)


def _nsa_prompt(cqt, small, cmp, ckv, ckvt, gn_cols, batch, seq):
    assert seq % (2 * NSA_TK) == 0, "key tiles are processed in pairs"
    nq = seq // TQ
    nb = seq // CMP_BLOCK
    row = lambda b, i: (b * nq + i, 0)
    kern = functools.partial(_nsa_prompt_kernel, n_blocks=nb)
    return pl.pallas_call(
        kern,
        grid=(batch, nq),
        in_specs=[pl.BlockSpec((W_C, TQ), lambda b, i: (0, b * nq + i)), pl.BlockSpec((TQ, LANES), row),
                  pl.BlockSpec((nb, LANES), lambda b, i: (b, 0)),
                  pl.BlockSpec((seq, 4 * DH_C), lambda b, i: (b, 0)),
                  pl.BlockSpec((4 * DH_C, seq), lambda b, i: (0, b)),
                  pl.BlockSpec((DH_C, H_C), lambda b, i: (0, 0))],
        out_specs=pl.BlockSpec((TQ, W_C), row),
        out_shape=jax.ShapeDtypeStruct((batch * seq, W_C), BF16),
        scratch_shapes=[pltpu.VMEM((seq // NSA_TK, NSA_TK, H_C * TQ), F32)],
        compiler_params=_cparams(("parallel", "arbitrary")),
        name="nsa_prompt",
    )(cqt, small, cmp, ckv, ckvt, gn_cols)


def _nsa_decode_kernel(pt_ref, q_ref, small_ref, cur_nsa_ref, cur_win_ref, win_ref, kn0_ref, gn_ref, member_ref,
                       cache_ref, o_ref, new_win_ref, buf_ref, sem_ref, *, layer, n_pages, page):
    slot = _paged_prefetch(pt_ref, cache_ref, buf_ref, sem_ref, layer, n_pages, page)
    group = buf_ref.shape[1]
    _interleave([_nsa_decode_one(pl.program_id(0) * group + g, buf_ref.at[slot, g], win_ref.at[g], q_ref,
                                 small_ref, cur_nsa_ref, cur_win_ref, kn0_ref, gn_ref, member_ref, o_ref,
                                 new_win_ref.at[g], n_pages * page) for g in range(group)])


def _nsa_decode_one(s, past_ref, win_ref, q_ref, small_ref, cur_nsa_ref, cur_win_ref, kn0_ref, gn_ref, member_ref,
                    o_ref, new_win_ref, n_past):
    n_cmp = n_past // CMP_BLOCK
    cur = n_past // SEL_BLOCK
    t_f = float(n_past)
    hr = SUBLANES

    qrow = q_ref[pl.ds(s, 1), :]
    q8 = jnp.concatenate([qrow[:, h * DH_C:(h + 1) * DH_C] for h in range(H_C)]
                         + [jnp.zeros((hr - H_C, DH_C), F32)], axis=0).astype(BF16)
    slope = _alibi_slopes2((hr, 1), 0, 1)

    summ = _mm(past_ref[0:2 * DH_C, :].astype(BF16), member_ref[...], _NT)[:, 0:n_cmp] * (1.0 / CMP_BLOCK)
    yield
    km = summ[0:DH_C]
    ms = jnp.sum(km * km, axis=0, keepdims=True) * (1.0 / DH_C)
    kcb = (km * lax.rsqrt(ms + EPS) * kn0_ref[...]).astype(BF16)
    vcb = summ[DH_C:2 * DH_C].astype(BF16)
    blk = lax.broadcasted_iota(jnp.int32, (1, n_cmp), 1)
    blk_mid = (blk * CMP_BLOCK).astype(F32) + 0.5 * (CMP_BLOCK - 1)
    s_c = _mm(q8, kcb) - slope * (t_f - blk_mid)
    yield
    p_c = _masked_softmax2(s_c, blk >= 0, -1)
    o_c = _mm(p_c.astype(BF16), vcb, _NT)

    hrow = lax.broadcasted_iota(jnp.int32, (hr, n_cmp), 0)
    imp = jnp.sum(jnp.where(hrow < H_C, p_c, 0.0), axis=0, keepdims=True)
    imp = jnp.concatenate([imp, jnp.zeros((1, LANES - n_cmp), F32)], axis=1)
    jb = lax.broadcasted_iota(jnp.int32, (1, LANES), 1)
    forced = (jb == 0) | (jb == cur) | (jb == cur - 1)
    score = jnp.where(jb > cur, -1.0, jnp.where(forced, 2.0 * H_C, imp))
    srow = jnp.broadcast_to(score, (LANES, LANES))
    scol = srow.T
    ri = lax.broadcasted_iota(jnp.int32, (LANES, LANES), 0)
    ci = lax.broadcasted_iota(jnp.int32, (LANES, LANES), 1)
    ahead = (scol > srow) | ((scol == srow) & (ri < ci))
    rank = jnp.sum(ahead.astype(F32), axis=0, keepdims=True)
    sel = ((rank < float(min(TOP_K_BLOCKS, cur + 1))) & (jb <= cur)).astype(BF16)
    yield
    sel8 = jnp.broadcast_to(sel, (hr, LANES))
    picked = _mm(sel8, member_ref[...]) > 0.5

    ks = past_ref[2 * DH_C:3 * DH_C, :].astype(BF16)
    vs = past_ref[3 * DH_C:4 * DH_C, :].astype(BF16)
    pos = lax.broadcasted_iota(jnp.int32, (1, n_past), 1).astype(F32)
    s_s = _mm(q8, ks) - slope * (t_f - pos)
    cur_row = cur_nsa_ref[pl.ds(s, 1), :]
    q8f = q8.astype(F32)
    s_cur = jnp.sum(q8f * cur_row[:, 2 * DH_C:3 * DH_C].astype(BF16).astype(F32), axis=-1, keepdims=True)
    yield
    s_s = jnp.where(picked, s_s, NEG_INF)
    m = jnp.maximum(jnp.max(s_s, axis=-1, keepdims=True), s_cur)
    e = jnp.where(picked, jnp.exp2(s_s - m), 0.0)
    e_cur = jnp.exp2(s_cur - m)
    den = jnp.sum(e, axis=-1, keepdims=True) + e_cur
    o_s = (_mm(e.astype(BF16), vs, _NT) + e_cur * cur_row[:, 3 * DH_C:4 * DH_C]) / den

    yield
    n_buf = win_ref.shape[1]
    kw = win_ref[0:DH_C, :].astype(BF16)
    vw = win_ref[DH_C:2 * DH_C, :].astype(BF16)
    wpos = n_past - n_buf + lax.broadcasted_iota(jnp.int32, (1, n_buf), 1)
    vis_w = (n_past - wpos < WINDOW) & (wpos >= 0)
    s_w = _mm(q8, kw) - slope * (t_f - wpos.astype(F32))
    cur_w = cur_win_ref[pl.ds(s, 1), :]
    sw_cur = jnp.sum(q8f * cur_w[:, 0:DH_C].astype(BF16).astype(F32), axis=-1, keepdims=True)
    yield
    s_w = jnp.where(vis_w, s_w, NEG_INF)
    m = jnp.maximum(jnp.max(s_w, axis=-1, keepdims=True), sw_cur)
    e = jnp.where(vis_w, jnp.exp2(s_w - m), 0.0)
    e_cur = jnp.exp2(sw_cur - m)
    den = jnp.sum(e, axis=-1, keepdims=True) + e_cur
    o_w = (_mm(e.astype(BF16), vw, _NT) + e_cur * cur_w[:, DH_C:2 * DH_C]) / den
    cur_cols = jnp.broadcast_to(cur_w, (LANES, LANES)).T
    cur_cols = jnp.concatenate([cur_cols] * (n_buf // LANES), axis=1)
    tok = lax.broadcasted_iota(jnp.int32, (2 * DH_C, n_buf), 1)
    new_win_ref[...] = jnp.where(tok == n_buf - 1, cur_cols, pltpu.roll(win_ref[...], n_buf - 1, 1))

    yield
    gates = jnp.broadcast_to(_sigmoid(small_ref[pl.ds(s, 1), :]), (hr, LANES))
    gl = lax.broadcasted_iota(jnp.int32, (hr, LANES), 1)
    gh = lax.broadcasted_iota(jnp.int32, (hr, LANES), 0)

    def gate_col(rr):
        return jnp.sum(jnp.where(gl == SMALL_GATE + 3 * gh + rr, gates, 0.0), axis=-1, keepdims=True)

    o = gate_col(0) * o_c + gate_col(1) * o_s + gate_col(2) * o_w
    o = o * lax.rsqrt(jnp.mean(o * o, axis=-1, keepdims=True) + EPS)
    orow = jnp.concatenate([o[h:h + 1] for h in range(H_C)], axis=1)
    o_ref[pl.ds(s, 1), :] = orow * gn_ref[...]


def _nsa_decode(page_table, cq, small, cur_nsa, cur_win, state_win, kn0, gn, cache, layer):
    n_seq, n_pages = page_table.shape
    page = cache.shape[3]
    n_buf = state_win.shape[3]
    n_past = n_pages * page
    assert CMP_BLOCK == SEL_BLOCK and n_past // SEL_BLOCK < LANES
    member = (lax.broadcasted_iota(jnp.int32, (LANES, n_past), 0)
              == lax.broadcasted_iota(jnp.int32, (LANES, n_past), 1) // SEL_BLOCK).astype(BF16)
    kern = functools.partial(_nsa_decode_kernel, layer=layer, n_pages=n_pages, page=page)
    whole = lambda shape: pl.BlockSpec(shape, lambda s, pt: (0,) * len(shape))
    return pl.pallas_call(
        kern,
        grid_spec=pltpu.PrefetchScalarGridSpec(
            num_scalar_prefetch=1,
            grid=(n_seq // NSA_DEC_GROUP,),
            in_specs=[whole((n_seq, W_C)), whole((n_seq, LANES)), whole((n_seq, 4 * DH_C)), whole((n_seq, 2 * DH_C)),
                      pl.BlockSpec((NSA_DEC_GROUP, None, 2 * DH_C, n_buf), lambda s, pt: (s, layer, 0, 0)),
                      whole((DH_C, 1)), whole((1, W_C)), whole((LANES, n_past)),
                      pl.BlockSpec(memory_space=pl.ANY)],
            out_specs=[whole((n_seq, W_C)),
                       pl.BlockSpec((NSA_DEC_GROUP, 2 * DH_C, n_buf), lambda s, pt: (s, 0, 0))],
            scratch_shapes=[pltpu.VMEM((2, NSA_DEC_GROUP, 4 * DH_C, n_pages * page), F32),
                            pltpu.SemaphoreType.DMA((2,))]),
        out_shape=[jax.ShapeDtypeStruct((n_seq, W_C), F32),
                   jax.ShapeDtypeStruct((n_seq, 2 * DH_C, n_buf), F32)],
        compiler_params=_cparams(("arbitrary",)),
        name="nsa_decode",
    )(page_table, cq, small, cur_nsa, cur_win, state_win, kn0, gn, member, cache)


def _pack_w_in(w_in):
    wt = jnp.transpose(w_in, (2, 0, 1))
    widths = [W_A, W_A, W_A, H_B * DK_B, H_B * DK_B, W_B, W_B, H_B, H_B, W_C] + [DH_C] * 6 + [3 * H_C]
    rows, start = [], 0
    for w in widths:
        rows.append(wt[start:start + w])
        start += w
    (a_q, a_k, a_v, b_q, b_k, b_v, b_z, b_beta, b_decay, c_q, c_kc, c_vc, c_ks, c_vs, c_kw, c_vw, c_gate) = rows
    pad = jnp.zeros((LANES - 2 * H_B - 3 * H_C,) + wt.shape[1:], wt.dtype)
    packed = jnp.concatenate([a_k, a_v, a_q, b_q, b_k, b_v, b_z, c_q, c_kc, c_vc, c_ks, c_vs, c_kw, c_vw,
                              b_beta, b_decay, c_gate, pad], axis=0)
    return jnp.transpose(packed.astype(BF16), (1, 0, 2))


def _row_tile(t, cap):
    tm = min(t, cap)
    while t % tm:
        tm //= 2
    return tm


def kernel(x_prompt, x_sample, cache_sb, cache_nsa, state_win, state_delta, state_conv, page_table, w_in, conv_w,
           a_log, dt_bias, b_out_norm, a_out_norm, c_q_norm, c_k_norm, c_out_norm, w_out, attn_norm, ffn_norm,
           dense_w_gu, dense_w_down, router_w, router_b, moe_w_gu, moe_w_down):
    batch, seq, _ = x_prompt.shape
    n_dec = x_sample.shape[0]
    depth = w_in.shape[0]
    n_phys, _, page = cache_sb.shape[:3]
    n_buf = state_win.shape[2]
    tp, ts = batch * seq, n_dec
    tm_p, tm_s = _row_tile(tp, 512), _row_tile(ts, 512)

    cache_sb2 = jnp.transpose(cache_sb, (0, 1, 3, 4, 5, 2)).reshape(n_phys, depth, 2 * W_A, page)
    cache_nsa2 = jnp.transpose(cache_nsa, (0, 1, 3, 4, 2)).reshape(n_phys, depth, 4 * DH_C, page)
    state_win2 = jnp.transpose(state_win, (0, 1, 3, 4, 2)).reshape(n_dec, depth, 2 * DH_C, n_buf)
    d_ff = dense_w_down.shape[1]
    tf = d_ff // 2 if (d_ff // 2) % LANES == 0 else d_ff

    xp = x_prompt.reshape(tp, D_MODEL)
    xs = x_sample.reshape(ts, D_MODEL)
    outs_p = {k: [] for k in ("sb", "nsa", "win", "delta", "conv")}
    outs_s = {k: [] for k in ("sb", "nsa", "win", "delta", "conv")}
    w_pk_all = _pack_w_in(w_in)
    for l in range(depth):
        w_pk = w_pk_all[l]
        wo = w_out[l].astype(BF16)
        g_attn = attn_norm[l].reshape(1, D_MODEL)
        g_ffn = ffn_norm[l].reshape(1, D_MODEL)
        qn = jnp.tile(c_q_norm[l], H_C).reshape(1, W_C)
        ones64 = jnp.ones((DH_C,), F32)
        kn = jnp.concatenate([ones64, ones64, c_k_norm[l, 1], ones64, c_k_norm[l, 2], ones64]).reshape(1, 384)
        kn0 = jnp.concatenate([c_k_norm[l, 0], ones64]).reshape(1, LANES)
        par = (jnp.zeros((2, LANES), F32).at[0, SMALL_DECAY:SMALL_DECAY + H_B].set(a_log[l])
               .at[1, SMALL_DECAY:SMALL_DECAY + H_B].set(dt_bias[l]))
        gn_a = a_out_norm[l].reshape(1, W_A)
        gn_b = b_out_norm[l].reshape(1, DV_B)
        gn_c = c_out_norm[l].reshape(1, W_C)
        cw = conv_w[l]

        (_, aq, akv, braw, bz, small, _, nsa, win, ckv, cqt, ckvt, sbt, nsat) = _inproj(xp, g_attn, w_pk, qn, kn, tm_p)
        oa = _sb_prompt(aq, akv, gn_a, batch, seq)
        ob, s_fin = _dn_prompt(braw, bz, small, cw, par, gn_b, batch, seq)
        cmp = _nsa_cmp(nsa, kn0, batch, seq)
        oc = _nsa_prompt(cqt, small, cmp, ckv, ckvt, c_out_norm[l].reshape(H_C, DH_C).T, batch, seq)
        outs_p["sb"].append(jnp.transpose(sbt.reshape(2, H_A, DH_A, batch, seq), (3, 4, 0, 1, 2)))
        outs_p["nsa"].append(jnp.transpose(nsat.reshape(4, DH_C, batch, seq), (2, 3, 0, 1)))
        outs_p["win"].append(win.reshape(batch, seq, 2, DH_C)[:, seq - min(WINDOW, seq):])
        outs_p["delta"].append(s_fin)
        outs_p["conv"].append(braw.reshape(batch, seq, CONV_CH)[:, seq - (CONV_W - 1):])

        sb_s, aq_s, _, braw_s, bz_s, small_s, cq_s, nsa_s, win_s, _, _, _, _, _ = _inproj(xs, g_attn, w_pk, qn, kn, tm_s)
        oa_s = _sb_decode(page_table, aq_s.astype(F32), gn_a, cache_sb2, l).astype(BF16)
        ob_s, s_new = _dn_decode(braw_s, bz_s, small_s, state_conv, state_delta, cw, par, gn_b, l)
        oc_s, new_win = _nsa_decode(page_table, cq_s.astype(F32), small_s, nsa_s, win_s, state_win2,
                                    c_k_norm[l, 0].reshape(DH_C, 1), gn_c, cache_nsa2, l)
        oc_s = oc_s.astype(BF16)
        outs_s["sb"].append(sb_s.reshape(n_dec, 1, 2, H_A, DH_A))
        outs_s["nsa"].append(nsa_s.reshape(n_dec, 1, 4, DH_C))
        outs_s["win"].append(jnp.transpose(new_win.reshape(n_dec, 2, DH_C, n_buf), (0, 3, 1, 2)))
        outs_s["delta"].append(s_new)
        outs_s["conv"].append(jnp.concatenate([state_conv[:, l], braw_s.reshape(n_dec, 1, CONV_CH)], axis=1)[:, 1:])

        if l % 2 == 0:
            wgu = dense_w_gu[l // 2].astype(BF16)
            wd = dense_w_down[l // 2].astype(BF16)
            xp = _ffn_dense(xp, oa, ob, oc, wo, g_ffn, wgu, wd, tm_p, tf)
            xs = _ffn_dense(xs, oa_s, ob_s, oc_s, wo, g_ffn, wgu, wd, tm_s, tf)
        else:
            wr = jnp.pad(router_w[l // 2], ((0, 0), (0, LANES - N_EXPERTS)))
            br = jnp.pad(router_b[l // 2], (0, LANES - N_EXPERTS)).reshape(1, LANES)
            wgu = moe_w_gu[l // 2].astype(BF16)
            wd = moe_w_down[l // 2].astype(BF16)
            xp = _ffn_moe(xp, oa, ob, oc, wo, g_ffn, wr, br, wgu, wd, tm_p)
            xs = _ffn_moe(xs, oa_s, ob_s, oc_s, wo, g_ffn, wr, br, wgu, wd, tm_s)

    stack = lambda xs_: jnp.stack(xs_, axis=1)
    return (xp.reshape(batch, seq, D_MODEL), xs.reshape(n_dec, 1, D_MODEL),
            stack(outs_p["sb"]), stack(outs_s["sb"]), stack(outs_p["nsa"]), stack(outs_s["nsa"]),
            stack(outs_p["win"]), stack(outs_s["win"]), stack(outs_p["delta"]), stack(outs_s["delta"]),
            stack(outs_p["conv"]), stack(outs_s["conv"]))
```
